```python
import jax, jax.numpy as jnp
from jax import lax
import numpy as np

D_MODEL = 1024
BATCH = 8
SEQ = 4096
DEPTH = 2

GRID_W = 64
CTX_LEN = 256
EPS = 1e-6
HEAD_DIM = 64
CONV_WIDTH = 512
CONV_K = 3
NA_HEADS = 8
NA_WIDTH = NA_HEADS * HEAD_DIM
NA_KH = 8
NA_KW = 16
GQA_Q_HEADS = 8
GQA_KV_HEADS = 2
GQA_WIDTH = GQA_Q_HEADS * HEAD_DIM
GQA_KV_WIDTH = GQA_KV_HEADS * HEAD_DIM
ROPE_THETA = 10000.0
Q_BLOCK = 128
N_EXPERTS = 32
TOP_K = 4
D_FF_EXPERT = D_MODEL
SWIGLU_ALPHA = 1.702
SWIGLU_LIMIT = 7.0
MOE_BLOCK = 128
IN_SPLITS = (CONV_WIDTH,) * 3 + (NA_WIDTH,) * 3 + (GQA_WIDTH, GQA_KV_WIDTH, GQA_KV_WIDTH) + (D_MODEL,) * 3
IN_COLS = sum(IN_SPLITS)
IN_OFFSETS = [int(o) for o in np.cumsum(IN_SPLITS)[:-1]]

kernel_name = 'hybrid_conv_na_gqa_moe_dit'


def rmsnorm(x, g):
    xf = x.astype(jnp.float32)
    y = xf * lax.rsqrt(jnp.mean(xf * xf, axis=-1, keepdims=True) + EPS)
    return (y * g.astype(jnp.float32)).astype(x.dtype)


def modulate(h, shift, scale):
    return h * (1 + scale) + shift


def split_in(p):
    return jnp.split(p, IN_OFFSETS, axis=-1)


def heads(t, h):
    return t.reshape(t.shape[:-1] + (h, HEAD_DIM))


def conv3_centered(u, w):
    up = jnp.pad(u, ((0, 0), (1, 1), (0, 0)))
    return up[:, :-2] * w[0] + up[:, 1:-1] * w[1] + up[:, 2:] * w[2]


def axial_rope(x):
    n, dh = x.shape[1], x.shape[-1]
    half = dh // 2
    t = jnp.arange(n)
    inv = ROPE_THETA ** (-jnp.arange(0, half, 2, dtype=jnp.float32) / half)

    def rot(xa, pos):
        ang = pos.astype(jnp.float32)[:, None] * inv
        cos = jnp.cos(ang)[None, :, None, :]
        sin = jnp.sin(ang)[None, :, None, :]
        x1, x2 = jnp.split(xa.astype(jnp.float32), 2, axis=-1)
        return jnp.concatenate([x1 * cos - x2 * sin, x1 * sin + x2 * cos], axis=-1)

    xr, xc = jnp.split(x, 2, axis=-1)
    return jnp.concatenate([rot(xr, t // GRID_W), rot(xc, t % GRID_W)], axis=-1).astype(x.dtype)


def attend(q, k, v):
    b, t, hq, dh = q.shape
    hkv = k.shape[2]
    qg = q.reshape(b, t, hkv, hq // hkv, dh)
    s = jnp.einsum('btkgd,bskd->bkgts', qg, k).astype(jnp.float32) * (dh ** -0.5)
    p = jax.nn.softmax(s, axis=-1).astype(v.dtype)
    o = jnp.einsum('bkgts,bskd->btkgd', p, v)
    return o.reshape(b, t, hq * dh)


def gqa_blocked(q, k, v, kc, vc):
    b, n, hq, dh = q.shape
    k_all = jnp.concatenate([k, kc], axis=1)
    v_all = jnp.concatenate([v, vc], axis=1)
    nb = n // Q_BLOCK
    qb = jnp.moveaxis(q.reshape(b, nb, Q_BLOCK, hq, dh), 1, 0)
    out = lax.map(lambda qi: attend(qi, k_all, v_all), qb)
    return jnp.moveaxis(out, 0, 1).reshape(b, n, hq * dh)


def neighborhood_attention(q, k, v, kc, vc, rpb):
    b, n, h, dh = q.shape
    rows = n // GRID_W
    wh = min(NA_KH, rows)
    qg = q.reshape(b, rows, GRID_W, h, dh)
    kg = k.reshape(b, rows, GRID_W, h, dh)
    vg = v.reshape(b, rows, GRID_W, h, dh)
    cols = jnp.arange(GRID_W)
    col_start = jnp.clip(cols - NA_KW // 2, 0, GRID_W - NA_KW)
    cidx = col_start[:, None] + jnp.arange(NA_KW)
    coff = cidx - cols[:, None] + (NA_KW - 1)
    scale = dh ** -0.5
    n_win = wh * NA_KW

    def row_block(r):
        rs = jnp.clip(r - NA_KH // 2, 0, rows - wh)
        q_r = lax.dynamic_index_in_dim(qg, r, axis=1, keepdims=False)
        k_band = lax.dynamic_slice_in_dim(kg, rs, wh, axis=1)
        v_band = lax.dynamic_slice_in_dim(vg, rs, wh, axis=1)
        k_win = k_band[:, :, cidx]
        v_win = v_band[:, :, cidx]
        roff = rs + jnp.arange(wh) - r + (NA_KH - 1)
        bias = rpb[:, roff[None, :, None], coff[:, None, :]]
        s_win = jnp.einsum('bchd,bicjhd->bhcij', q_r, k_win).astype(jnp.float32) * scale + bias.astype(jnp.float32)[None]
        s_ctx = jnp.einsum('bchd,blhd->bhcl', q_r, kc).astype(jnp.float32) * scale
        s = jnp.concatenate([s_win.reshape(b, h, GRID_W, n_win), s_ctx], axis=-1)
        p = jax.nn.softmax(s, axis=-1).astype(v.dtype)
        p_win = p[..., :n_win].reshape(b, h, GRID_W, wh, NA_KW)
        p_ctx = p[..., n_win:]
        return jnp.einsum('bhcij,bicjhd->bchd', p_win, v_win) + jnp.einsum('bhcl,blhd->bchd', p_ctx, vc)

    out = lax.map(row_block, jnp.arange(rows))
    return jnp.moveaxis(out, 0, 1).reshape(b, n, h * dh)


def branch_merge(y_conv, y_na, y_gqa, g_conv, g_na, g_gqa, w_br_conv, w_br_na, w_br_gqa, w_out):
    m = (jax.nn.sigmoid(g_conv) * (y_conv @ w_br_conv)
         + jax.nn.sigmoid(g_na) * (y_na @ w_br_na)
         + jax.nn.sigmoid(g_gqa) * (y_gqa @ w_br_gqa))
    return m @ w_out


def clamped_swiglu(hgu):
    g = jnp.minimum(hgu[..., ::2], SWIGLU_LIMIT)
    u = jnp.clip(hgu[..., 1::2], -SWIGLU_LIMIT, SWIGLU_LIMIT)
    return (u + 1) * (g * jax.nn.sigmoid(SWIGLU_ALPHA * g))


def moe(h, router_w, router_b, w_gu, b_gu, w_down, b_down):
    n, d = h.shape
    logits = (h @ router_w + router_b).astype(jnp.float32)
    top_val, top_idx = lax.top_k(logits, TOP_K)
    gates = jax.nn.softmax(top_val, axis=-1).astype(h.dtype)
    nk = n * TOP_K
    flat_e = top_idx.reshape(nk)
    flat_tok = jnp.repeat(jnp.arange(n, dtype=jnp.int32), TOP_K)
    flat_g = gates.reshape(nk)
    order = jnp.argsort(flat_e)
    se, stok, sg = flat_e[order], flat_tok[order], flat_g[order]
    counts = jnp.bincount(flat_e, length=N_EXPERTS)
    padded = (counts + MOE_BLOCK - 1) // MOE_BLOCK * MOE_BLOCK
    pend = jnp.cumsum(padded)
    pstart = pend - padded
    cstart = jnp.cumsum(counts) - counts
    dest = pstart[se] + jnp.arange(nk) - cstart[se]
    n_blocks = -(-nk // MOE_BLOCK) + N_EXPERTS
    cap = n_blocks * MOE_BLOCK
    tok_buf = jnp.zeros((cap,), jnp.int32).at[dest].set(stok)
    gate_buf = jnp.zeros((cap,), h.dtype).at[dest].set(sg)
    block_e = jnp.minimum(jnp.searchsorted(pend, jnp.arange(n_blocks) * MOE_BLOCK, side='right'), N_EXPERTS - 1)

    def expert_block(args):
        tok, g, e = args
        xb = h[tok]
        a = clamped_swiglu(xb @ w_gu[e] + b_gu[e])
        return (a @ w_down[e] + b_down[e]) * g[:, None]

    y = lax.map(expert_block, (tok_buf.reshape(n_blocks, MOE_BLOCK), gate_buf.reshape(n_blocks, MOE_BLOCK), block_e))
    return jnp.zeros_like(h).at[tok_buf].add(y.reshape(cap, d))


def setup_inputs(seed: int = 0) -> dict:
    key = jax.random.key(seed)
    ks = jax.random.split(key, 24)

    def nrm(k, shape, scale):
        return jax.random.normal(k, shape, jnp.float32) * scale

    return {
        'x': nrm(ks[0], (BATCH, SEQ, D_MODEL), 1.0),
        'c': nrm(ks[1], (BATCH, D_MODEL), 1.0),
        'ctx': nrm(ks[2], (BATCH, CTX_LEN, D_MODEL), 1.0),
        'c_ctx': nrm(ks[3], (D_MODEL,), 1.0),
        'ada_w': nrm(ks[4], (DEPTH, D_MODEL, 6 * D_MODEL), 0.5 * D_MODEL ** -0.5),
        'ada_b': nrm(ks[5], (DEPTH, 6 * D_MODEL), 0.02),
        'norm1_g': 1.0 + nrm(ks[6], (DEPTH, D_MODEL), 0.02),
        'norm2_g': 1.0 + nrm(ks[7], (DEPTH, D_MODEL), 0.02),
        'w_in': nrm(ks[8], (DEPTH, D_MODEL, IN_COLS), D_MODEL ** -0.5),
        'conv_w': nrm(ks[9], (DEPTH, CONV_K, CONV_WIDTH), CONV_K ** -0.5),
        'na_rpb': nrm(ks[10], (DEPTH, NA_HEADS, 2 * NA_KH - 1, 2 * NA_KW - 1), 0.1),
        'q_norm_g': 1.0 + nrm(ks[11], (DEPTH, HEAD_DIM), 0.02),
        'k_norm_g': 1.0 + nrm(ks[12], (DEPTH, HEAD_DIM), 0.02),
        'w_br_conv': nrm(ks[13], (DEPTH, CONV_WIDTH, D_MODEL), CONV_WIDTH ** -0.5),
        'w_br_na': nrm(ks[14], (DEPTH, NA_WIDTH, D_MODEL), NA_WIDTH ** -0.5),
        'w_br_gqa': nrm(ks[15], (DEPTH, GQA_WIDTH, D_MODEL), GQA_WIDTH ** -0.5),
        'w_out': nrm(ks[16], (DEPTH, D_MODEL, D_MODEL), D_MODEL ** -0.5),
        'router_w': nrm(ks[17], (DEPTH, D_MODEL, N_EXPERTS), D_MODEL ** -0.5),
        'router_b': nrm(ks[18], (DEPTH, N_EXPERTS), 0.01),
        'w_gu': nrm(ks[19], (DEPTH, N_EXPERTS, D_MODEL, 2 * D_FF_EXPERT), D_MODEL ** -0.5),
        'b_gu': nrm(ks[20], (DEPTH, N_EXPERTS, 2 * D_FF_EXPERT), 0.01),
        'w_down': nrm(ks[21], (DEPTH, N_EXPERTS, D_FF_EXPERT, D_MODEL), D_FF_EXPERT ** -0.5),
        'b_down': nrm(ks[22], (DEPTH, N_EXPERTS, D_MODEL), 0.01),
        'final_g': 1.0 + nrm(ks[23], (D_MODEL,), 0.02),
    }


def reference(x, c, ctx, c_ctx, ada_w, ada_b, norm1_g, norm2_g, w_in, conv_w, na_rpb,
              q_norm_g, k_norm_g, w_br_conv, w_br_na, w_br_gqa, w_out,
              router_w, router_b, w_gu, b_gu, w_down, b_down, final_g):
    b, n, d = x.shape
    n_ctx = ctx.shape[1]
    xl, xc = x, ctx
    for l in range(DEPTH):
        last = l == DEPTH - 1
        mod_l = (jax.nn.silu(c) @ ada_w[l] + ada_b[l])[:, None, :]
        mod_c = jax.nn.silu(c_ctx) @ ada_w[l] + ada_b[l]
        sh1_l, sc1_l, gt1_l, sh2_l, sc2_l, gt2_l = jnp.split(mod_l, 6, axis=-1)
        sh1_c, sc1_c, gt1_c, sh2_c, sc2_c, gt2_c = jnp.split(mod_c, 6, axis=-1)

        hl = modulate(rmsnorm(xl, norm1_g[l]), sh1_l, sc1_l)
        hc = modulate(rmsnorm(xc, norm1_g[l]), sh1_c, sc1_c)
        (cb_l, cc_l, cx_l, naq_l, nak_l, nav_l, gq_l, gk_l, gv_l,
         bg_conv_l, bg_na_l, bg_gqa_l) = split_in(hl @ w_in[l])
        (cb_c, cc_c, cx_c, naq_c, nak_c, nav_c, gq_c, gk_c, gv_c,
         bg_conv_c, bg_na_c, bg_gqa_c) = split_in(hc @ w_in[l])
        na_kc = heads(nak_c, NA_HEADS)
        na_vc = heads(nav_c, NA_HEADS)
        gqa_kc = rmsnorm(heads(gk_c, GQA_KV_HEADS), k_norm_g[l])
        gqa_vc = heads(gv_c, GQA_KV_HEADS)

        y_conv = cb_l * conv3_centered(cc_l * cx_l, conv_w[l])
        y_na = neighborhood_attention(heads(naq_l, NA_HEADS), heads(nak_l, NA_HEADS),
                                      heads(nav_l, NA_HEADS), na_kc, na_vc, na_rpb[l])
        q_l = axial_rope(rmsnorm(heads(gq_l, GQA_Q_HEADS), q_norm_g[l]))
        k_l = axial_rope(rmsnorm(heads(gk_l, GQA_KV_HEADS), k_norm_g[l]))
        y_gqa = gqa_blocked(q_l, k_l, heads(gv_l, GQA_KV_HEADS), gqa_kc, gqa_vc)
        xl = xl + gt1_l * branch_merge(y_conv, y_na, y_gqa, bg_conv_l, bg_na_l, bg_gqa_l,
                                       w_br_conv[l], w_br_na[l], w_br_gqa[l], w_out[l])

        if not last:
            yc_conv = cb_c * conv3_centered(cc_c * cx_c, conv_w[l])
            yc_na = attend(heads(naq_c, NA_HEADS), na_kc, na_vc)
            yc_gqa = attend(rmsnorm(heads(gq_c, GQA_Q_HEADS), q_norm_g[l]), gqa_kc, gqa_vc)
            xc = xc + gt1_c * branch_merge(yc_conv, yc_na, yc_gqa, bg_conv_c, bg_na_c, bg_gqa_c,
                                           w_br_conv[l], w_br_na[l], w_br_gqa[l], w_out[l])

        hl2 = modulate(rmsnorm(xl, norm2_g[l]), sh2_l, sc2_l).reshape(b * n, d)
        if not last:
            hc2 = modulate(rmsnorm(xc, norm2_g[l]), sh2_c, sc2_c).reshape(b * n_ctx, d)
            out = moe(jnp.concatenate([hl2, hc2], axis=0), router_w[l], router_b[l],
                      w_gu[l], b_gu[l], w_down[l], b_down[l])
            xl = xl + gt2_l * out[:b * n].reshape(b, n, d)
            xc = xc + gt2_c * out[b * n:].reshape(b, n_ctx, d)
        else:
            out = moe(hl2, router_w[l], router_b[l], w_gu[l], b_gu[l], w_down[l], b_down[l])
            xl = xl + gt2_l * out.reshape(b, n, d)
    return rmsnorm(xl, final_g)
```

```python
import functools

import jax
import jax.numpy as jnp
import numpy as np
from jax import lax
from jax.experimental import pallas as pl
from jax.experimental.pallas import tpu as pltpu

F32 = jnp.float32
BF16 = jnp.bfloat16

GRID_W = 64
EPS = 1e-6
HEAD_DIM = 64
CONV_WIDTH = 512
NA_HEADS = 8
NA_WIDTH = NA_HEADS * HEAD_DIM
NA_KH = 8
NA_KW = 16
NA_QROWS = 4
NA_BAND = NA_QROWS + NA_KH
GQA_Q_HEADS = 8
GQA_KV_HEADS = 2
GQA_GROUP = GQA_Q_HEADS // GQA_KV_HEADS
GQA_WIDTH = GQA_Q_HEADS * HEAD_DIM
GQA_KV_WIDTH = GQA_KV_HEADS * HEAD_DIM
ROPE_THETA = 10000.0
TOP_K = 4
SWIGLU_ALPHA = 1.702
SWIGLU_LIMIT = 7.0
MOD_ROWS = 16
NEG_BIG = -1e30
VMEM_LIMIT = 56 * 1024 * 1024

TM = 512
TQ = 128
BM = 256
TG = 256


def _cparams(*sem):
    return pltpu.CompilerParams(dimension_semantics=sem, vmem_limit_bytes=VMEM_LIMIT)


def _rms(x):
    return x * lax.rsqrt(jnp.mean(x * x, axis=-1, keepdims=True) + EPS)


def _ada_kernel(c_ref, w_ref, b_ref, o_ref):
    c = c_ref[...]
    s = c * jax.nn.sigmoid(c)
    o_ref[0] = jnp.dot(s, w_ref[0], precision=lax.Precision.HIGHEST,
                       preferred_element_type=F32) + b_ref[0]


def ada_mods(cond, ada_w, ada_b):
    depth, d, n6 = ada_w.shape
    tn = d
    return pl.pallas_call(
        _ada_kernel,
        grid=(depth, n6 // tn),
        in_specs=[pl.BlockSpec((MOD_ROWS, d), lambda l, j: (0, 0)),
                  pl.BlockSpec((1, d, tn), lambda l, j: (l, 0, j)),
                  pl.BlockSpec((1, 1, tn), lambda l, j: (l, 0, j))],
        out_specs=pl.BlockSpec((1, MOD_ROWS, tn), lambda l, j: (l, 0, j)),
        out_shape=jax.ShapeDtypeStruct((depth, MOD_ROWS, n6), F32),
        compiler_params=_cparams("arbitrary", "arbitrary"),
        name="ada_mods",
    )(cond, ada_w, ada_b.reshape(depth, 1, n6))


def _head_inv_rms(t, hsum_ref):
    s = t * t
    s_hi = s.astype(BF16)
    s_lo = (s - s_hi.astype(F32)).astype(BF16)
    tot = (jnp.dot(s_hi, hsum_ref[...], preferred_element_type=F32)
           + jnp.dot(s_lo, hsum_ref[...], preferred_element_type=F32))
    return lax.rsqrt(tot * (1.0 / HEAD_DIM) + EPS)


def _rope(t, cos, sin):
    lane = lax.broadcasted_iota(jnp.int32, t.shape, 1)
    second = (lane % 32) >= 16
    partner = jnp.where(second, pltpu.roll(t, 16, 1), pltpu.roll(t, 128 - 16, 1))
    return t * cos + partner * sin


def _inproj_kernel(x_ref, sh_ref, sc_ref, g_ref, w_ref, cos_ref, sin_ref, qg_ref, kg_ref, hsum_ref,
                   conv_o, naq_o, nak_o, nav_o, gq_o, gk_o, gv_o, gate_o, h_scr, *, d):
    h = _rms(x_ref[...]) * g_ref[...]
    h = h * (1.0 + sc_ref[0]) + sh_ref[0]
    h_scr[...] = h.astype(BF16)

    def proj(a, n):
        return jnp.dot(h_scr[...], w_ref[:, a:a + n], preferred_element_type=F32)

    off = 0
    for j in range(3):
        conv_o[:, j * CONV_WIDTH:(j + 1) * CONV_WIDTH] = proj(off, CONV_WIDTH).astype(BF16)
        off += CONV_WIDTH
    naq_o[...] = (proj(off, NA_WIDTH) * (HEAD_DIM ** -0.5)).astype(BF16)
    off += NA_WIDTH
    nak_o[...] = proj(off, NA_WIDTH).astype(BF16)
    off += NA_WIDTH
    nav_o[...] = proj(off, NA_WIDTH).astype(BF16)
    off += NA_WIDTH
    cos = cos_ref[...]
    sin = sin_ref[...]
    for j in range(GQA_WIDTH // 128):
        t = proj(off + j * 128, 128)
        t = t * _head_inv_rms(t, hsum_ref) * qg_ref[...]
        gq_o[:, j * 128:(j + 1) * 128] = (_rope(t, cos, sin) * (HEAD_DIM ** -0.5)).astype(BF16)
    off += GQA_WIDTH
    t = proj(off, GQA_KV_WIDTH)
    t = t * _head_inv_rms(t, hsum_ref) * kg_ref[...]
    gk_o[...] = _rope(t, cos, sin).astype(BF16)
    off += GQA_KV_WIDTH
    gv_o[...] = proj(off, GQA_KV_WIDTH).astype(BF16)
    off += GQA_KV_WIDTH
    for j in range(3):
        gate_o[:, j * d:(j + 1) * d] = jax.nn.sigmoid(proj(off + j * d, d)).astype(BF16)


def in_projection(x_all, mods, g1, w_in_bf, cos_t, sin_t, qg, kg, hsum, *, seq, n_lat, n_batch):
    nt, d = x_all.shape
    tm = TM
    n_lat_tiles = n_lat // tm
    seq_tiles = seq // tm
    in_cols = w_in_bf.shape[1]

    def mod_row(i):
        return jnp.minimum((i * tm) // seq, n_batch)

    def rope_blk(i):
        return jnp.where(i < n_lat_tiles, i % seq_tiles, seq_tiles)

    tok = lambda w: pl.BlockSpec((tm, w), lambda i: (i, 0))
    const = lambda shape: pl.BlockSpec(shape, lambda i: tuple(0 for _ in shape))
    outs = [CONV_WIDTH * 3, NA_WIDTH, NA_WIDTH, NA_WIDTH, GQA_WIDTH, GQA_KV_WIDTH, GQA_KV_WIDTH, 3 * d]
    return pl.pallas_call(
        functools.partial(_inproj_kernel, d=d),
        grid=(nt // tm,),
        in_specs=[tok(d),
                  pl.BlockSpec((1, 1, d), lambda i: (mod_row(i), 0, 0)),
                  pl.BlockSpec((1, 1, d), lambda i: (mod_row(i), 0, 1)),
                  const((1, d)),
                  pl.BlockSpec((d, in_cols), lambda i: (0, 0), pipeline_mode=pl.Buffered(1)),
                  pl.BlockSpec((tm, 128), lambda i: (rope_blk(i), 0)),
                  pl.BlockSpec((tm, 128), lambda i: (rope_blk(i), 0)),
                  const((1, 128)), const((1, 128)), const((128, 128))],
        out_specs=[tok(w) for w in outs],
        out_shape=[jax.ShapeDtypeStruct((nt, w), BF16) for w in outs],
        scratch_shapes=[pltpu.VMEM((tm, d), BF16)],
        compiler_params=_cparams("arbitrary"),
        name="in_projection",
    )(x_all, mods, mods, g1, w_in_bf, cos_t, sin_t, qg, kg, hsum)


def _softmax_pv(scores, values):
    m = functools.reduce(jnp.maximum, [jnp.max(s, axis=-1, keepdims=True) for s in scores])
    ps = [jnp.exp(s - m) for s in scores]
    l = functools.reduce(lambda a, b: a + b, [jnp.sum(p, axis=-1, keepdims=True) for p in ps])
    o = functools.reduce(lambda a, b: a + b,
                         [jnp.dot(p.astype(BF16), v, preferred_element_type=F32) for p, v in zip(ps, values)])
    return o / l


def _qk(q, k):
    return lax.dot_general(q, k, (((1,), (1,)), ((), ())), preferred_element_type=F32)


def _na_kernel(q_ref, k_ref, v_ref, kc_ref, vc_ref, bias_ref, o_ref, *, w, rows):
    j = pl.program_id(1)
    start = jnp.clip(j * NA_QROWS - NA_KH // 2, 0, rows - NA_BAND) * w
    start = pl.multiple_of(start, w)
    nkeys = NA_BAND * w
    for h in range(NA_HEADS):
        hs = slice(h * HEAD_DIM, (h + 1) * HEAD_DIM)
        q = q_ref[:, hs]
        kw = k_ref[pl.ds(start, nkeys), hs]
        vw = v_ref[pl.ds(start, nkeys), hs]
        s_win = _qk(q, kw) + bias_ref[0, h]
        s_ctx = _qk(q, kc_ref[:, hs])
        o_ref[:, hs] = _softmax_pv([s_win, s_ctx], [vw, vc_ref[:, hs]]).astype(BF16)


def neighborhood_attention(naq, nak, nav, bias, *, seq, ctx, n_batch):
    w = GRID_W
    rows = seq // w
    nj = rows // NA_QROWS
    nq = NA_QROWS * w
    n_lat = n_batch * seq
    ctx_blk0 = n_lat // ctx

    def case(j):
        return jnp.where(j == 0, 0, jnp.where(j == nj - 1, 2, 1))

    return pl.pallas_call(
        functools.partial(_na_kernel, w=w, rows=rows),
        grid=(n_batch, nj),
        in_specs=[pl.BlockSpec((nq, NA_WIDTH), lambda b, j: (b * nj + j, 0)),
                  pl.BlockSpec((seq, NA_WIDTH), lambda b, j: (b, 0)),
                  pl.BlockSpec((seq, NA_WIDTH), lambda b, j: (b, 0)),
                  pl.BlockSpec((ctx, NA_WIDTH), lambda b, j: (ctx_blk0 + b, 0)),
                  pl.BlockSpec((ctx, NA_WIDTH), lambda b, j: (ctx_blk0 + b, 0)),
                  pl.BlockSpec((1, NA_HEADS, nq, NA_BAND * w), lambda b, j: (case(j), 0, 0, 0))],
        out_specs=pl.BlockSpec((nq, NA_WIDTH), lambda b, j: (b * nj + j, 0)),
        out_shape=jax.ShapeDtypeStruct((n_lat, NA_WIDTH), BF16),
        compiler_params=_cparams("arbitrary", "arbitrary"),
        name="neighborhood_attention",
    )(naq, nak, nav, nak, nav, bias)


def na_bias_table(rpb, seq):
    w = GRID_W
    rows = seq // w
    nj = rows // NA_QROWS
    cols = np.arange(w)
    col_start = np.clip(cols - NA_KW // 2, 0, w - NA_KW)
    tabs = []
    for j in (0, 1, nj - 1):
        start = int(np.clip(j * NA_QROWS - NA_KH // 2, 0, rows - NA_BAND))
        r = j * NA_QROWS + np.arange(NA_QROWS)
        rs = np.clip(r - NA_KH // 2, 0, rows - NA_KH)
        kr = start + np.arange(NA_BAND)
        row_ok = (kr[None, :] >= rs[:, None]) & (kr[None, :] < rs[:, None] + NA_KH)
        roff = np.clip(kr[None, :] - r[:, None] + (NA_KH - 1), 0, 2 * NA_KH - 2)
        col_ok = (cols[None, :] >= col_start[:, None]) & (cols[None, :] < col_start[:, None] + NA_KW)
        coff = np.clip(cols[None, :] - cols[:, None] + (NA_KW - 1), 0, 2 * NA_KW - 2)
        ok = row_ok[:, None, :, None] & col_ok[None, :, None, :]
        b = rpb[:, roff[:, None, :, None], coff[None, :, None, :]].astype(F32)
        b = jnp.where(jnp.asarray(ok)[None], b, NEG_BIG)
        tabs.append(b.reshape(rpb.shape[0], NA_QROWS * w, NA_BAND * w))
    return jnp.stack(tabs)


def _gqa_kernel(q_ref, kt_ref, v_ref, o_ref, *, tq):
    q = jnp.concatenate([q_ref[:, g * HEAD_DIM:(g + 1) * HEAD_DIM] for g in range(GQA_GROUP)], axis=0)
    s = jnp.dot(q, kt_ref[0, 0], preferred_element_type=F32)
    o = _softmax_pv([s], [v_ref[0, 0]])
    for g in range(GQA_GROUP):
        o_ref[:, g * HEAD_DIM:(g + 1) * HEAD_DIM] = o[g * tq:(g + 1) * tq].astype(BF16)


def gqa_attention(gq, kt_all, v_all, *, seq, n_batch):
    tq = TQ
    nq = seq // tq
    s_all = kt_all.shape[-1]
    gw = GQA_GROUP * HEAD_DIM
    return pl.pallas_call(
        functools.partial(_gqa_kernel, tq=tq),
        grid=(n_batch, GQA_KV_HEADS, nq),
        in_specs=[pl.BlockSpec((tq, gw), lambda b, kv, i: (b * nq + i, kv)),
                  pl.BlockSpec((1, 1, HEAD_DIM, s_all), lambda b, kv, i: (b, kv, 0, 0)),
                  pl.BlockSpec((1, 1, s_all, HEAD_DIM), lambda b, kv, i: (b, kv, 0, 0))],
        out_specs=pl.BlockSpec((tq, gw), lambda b, kv, i: (b * nq + i, kv)),
        out_shape=jax.ShapeDtypeStruct((n_batch * seq, GQA_WIDTH), BF16),
        compiler_params=_cparams("arbitrary", "arbitrary", "arbitrary"),
        name="gqa_attention",
    )(gq, kt_all, v_all)


def _ctx_attn_kernel(naq_ref, nak_ref, nav_ref, gq_ref, gk_ref, gv_ref, ona_ref, ogqa_ref):
    for h in range(NA_HEADS):
        hs = slice(h * HEAD_DIM, (h + 1) * HEAD_DIM)
        ona_ref[:, hs] = _softmax_pv([_qk(naq_ref[:, hs], nak_ref[:, hs])], [nav_ref[:, hs]]).astype(BF16)
    for h in range(GQA_Q_HEADS):
        hs = slice(h * HEAD_DIM, (h + 1) * HEAD_DIM)
        kv = h // GQA_GROUP
        ks = slice(kv * HEAD_DIM, (kv + 1) * HEAD_DIM)
        ogqa_ref[:, hs] = _softmax_pv([_qk(gq_ref[:, hs], gk_ref[:, ks])], [gv_ref[:, ks]]).astype(BF16)


def ctx_attention(naq, nak, nav, gq, gk, gv, *, ctx, n_lat, n_batch):
    blk0 = n_lat // ctx
    wide = lambda: pl.BlockSpec((ctx, NA_WIDTH), lambda b: (blk0 + b, 0))
    narrow = lambda: pl.BlockSpec((ctx, GQA_KV_WIDTH), lambda b: (blk0 + b, 0))
    out = lambda: pl.BlockSpec((ctx, NA_WIDTH), lambda b: (b, 0))
    return pl.pallas_call(
        _ctx_attn_kernel,
        grid=(n_batch,),
        in_specs=[wide(), wide(), wide(), wide(), narrow(), narrow()],
        out_specs=[out(), out()],
        out_shape=[jax.ShapeDtypeStruct((n_batch * ctx, NA_WIDTH), BF16)] * 2,
        compiler_params=_cparams("arbitrary"),
        name="ctx_attention",
    )(naq, nak, nav, gq, gk, gv)


def _merge_kernel(x_ref, gt_ref, conv_ref, prev_ref, next_ref, yna_ref, ygqa_ref, gate_ref,
                  cw_ref, wc_ref, wn_ref, wg_ref, wo_ref, o_ref, *, tm, d, seq, ctx, n_lat):
    i = pl.program_id(0)
    cw = CONV_WIDTH

    def u_of(ref, rows):
        return ref[rows, cw:2 * cw].astype(F32) * ref[rows, 2 * cw:3 * cw].astype(F32)

    u = u_of(conv_ref, slice(None))
    hp = u_of(prev_ref, slice(15, 16))
    hn = u_of(next_ref, slice(0, 1))
    row = lax.broadcasted_iota(jnp.int32, (tm, 1), 0)
    r = i * tm + row
    pos = jnp.where(r < n_lat, r % seq, r % ctx)
    length = jnp.where(r < n_lat, seq, ctx)
    u_prev = jnp.where(row == 0, hp, pltpu.roll(u, 1, 0))
    u_prev = jnp.where(pos == 0, 0.0, u_prev)
    u_next = jnp.where(row == tm - 1, hn, pltpu.roll(u, tm - 1, 0))
    u_next = jnp.where(pos == length - 1, 0.0, u_next)
    y_conv = conv_ref[:, 0:cw].astype(F32) * (u_prev * cw_ref[0:1] + u * cw_ref[1:2] + u_next * cw_ref[2:3])

    m = gate_ref[:, 0:d].astype(F32) * jnp.dot(y_conv.astype(BF16), wc_ref[...], preferred_element_type=F32)
    m += gate_ref[:, d:2 * d].astype(F32) * jnp.dot(yna_ref[...], wn_ref[...], preferred_element_type=F32)
    m += gate_ref[:, 2 * d:3 * d].astype(F32) * jnp.dot(ygqa_ref[...], wg_ref[...], preferred_element_type=F32)
    out = jnp.dot(m.astype(BF16), wo_ref[...], preferred_element_type=F32)
    o_ref[...] = x_ref[...] + gt_ref[0] * out


def merge_branches(x_all, mods, conv, y_na, y_gqa, gates, conv_w, wc, wn, wg, wo, *, n_rows, seq, ctx,
                   n_lat, n_batch):
    d = x_all.shape[1]
    tm = TM
    halo = 16
    n_halo_blocks = conv.shape[0] // halo
    per = tm // halo

    def mod_row(i):
        return jnp.minimum((i * tm) // seq, n_batch)

    tok = lambda w: pl.BlockSpec((tm, w), lambda i: (i, 0))
    const = lambda shape: pl.BlockSpec(shape, lambda i: tuple(0 for _ in shape))
    return pl.pallas_call(
        functools.partial(_merge_kernel, tm=tm, d=d, seq=seq, ctx=ctx, n_lat=n_lat),
        grid=(n_rows // tm,),
        in_specs=[tok(d),
                  pl.BlockSpec((1, 1, d), lambda i: (mod_row(i), 0, 2)),
                  tok(3 * CONV_WIDTH),
                  pl.BlockSpec((halo, 3 * CONV_WIDTH), lambda i: (jnp.maximum(i * per - 1, 0), 0)),
                  pl.BlockSpec((halo, 3 * CONV_WIDTH),
                               lambda i: (jnp.minimum((i + 1) * per, n_halo_blocks - 1), 0)),
                  tok(NA_WIDTH), tok(GQA_WIDTH), tok(3 * d),
                  const((3, CONV_WIDTH)), const((CONV_WIDTH, d)), const((NA_WIDTH, d)),
                  const((GQA_WIDTH, d)), const((d, d))],
        out_specs=tok(d),
        out_shape=jax.ShapeDtypeStruct((n_rows, d), F32),
        compiler_params=_cparams("arbitrary"),
        name="merge_branches",
    )(x_all, mods, conv, conv, conv, y_na, y_gqa, gates, conv_w, wc, wn, wg, wo)


def _router_kernel(x_ref, sh_ref, sc_ref, g_ref, rw_ref, rb_ref, tri_ref,
                   h_o, idx_o, gate_o, rank_o, cnt_o, carry, *, n_exp):
    i = pl.program_id(0)

    @pl.when(i == 0)
    def _():
        carry[...] = jnp.zeros_like(carry)

    h = _rms(x_ref[...]) * g_ref[...]
    h = h * (1.0 + sc_ref[0]) + sh_ref[0]
    h_o[...] = h
    logits = jnp.dot(h, rw_ref[...], precision=lax.Precision.HIGHEST,
                     preferred_element_type=F32) + rb_ref[...]
    lane = lax.broadcasted_iota(jnp.int32, logits.shape, 1)
    work = logits
    vals, idxs, hots = [], [], []
    for _ in range(TOP_K):
        m = jnp.max(work, axis=-1, keepdims=True)
        ik = jnp.min(jnp.where(work == m, lane, n_exp), axis=-1, keepdims=True)
        hot = lane == ik
        vals.append(m)
        idxs.append(ik)
        hots.append(hot)
        work = jnp.where(hot, -jnp.inf, work)
    es = [jnp.exp(v - vals[0]) for v in vals]
    den = functools.reduce(lambda a, b: a + b, es)
    hot_sum = functools.reduce(lambda a, b: a + b, [jnp.where(hh, 1.0, 0.0) for hh in hots])
    prefix = jnp.dot(tri_ref[...], hot_sum.astype(BF16), preferred_element_type=F32) + carry[...]
    ranks = [jnp.sum(jnp.where(hh, prefix, 0.0), axis=-1, keepdims=True) for hh in hots]
    carry[...] = carry[...] + jnp.sum(hot_sum, axis=0, keepdims=True)
    cnt_o[...] = carry[...].astype(jnp.int32)

    out_lane = lax.broadcasted_iota(jnp.int32, idx_o.shape, 1)

    def spread(cols, fill):
        acc = jnp.full(idx_o.shape, fill, cols[0].dtype)
        for k, c in enumerate(cols):
            acc = jnp.where(out_lane == k, c, acc)
        return acc

    idx_o[...] = spread(idxs, 0)
    gate_o[...] = spread([e / den for e in es], 0.0)
    rank_o[...] = spread([rk.astype(jnp.int32) for rk in ranks], 0)


def router(x_all, mods, g2, router_w, router_b, *, n_rows, seq, n_batch):
    d = x_all.shape[1]
    n_exp = router_w.shape[1]
    tm = TM
    tri = jnp.asarray(np.tril(np.ones((tm, tm), np.float32), -1), BF16)

    def mod_row(i):
        return jnp.minimum((i * tm) // seq, n_batch)

    tok = lambda w: pl.BlockSpec((tm, w), lambda i: (i, 0))
    const = lambda shape: pl.BlockSpec(shape, lambda i: tuple(0 for _ in shape))
    return pl.pallas_call(
        functools.partial(_router_kernel, n_exp=n_exp),
        grid=(n_rows // tm,),
        in_specs=[tok(d),
                  pl.BlockSpec((1, 1, d), lambda i: (mod_row(i), 0, 3)),
                  pl.BlockSpec((1, 1, d), lambda i: (mod_row(i), 0, 4)),
                  const((1, d)), const((d, n_exp)), const((1, n_exp)), const((tm, tm))],
        out_specs=[tok(d), tok(128), tok(128), tok(128), const((1, n_exp))],
        out_shape=[jax.ShapeDtypeStruct((n_rows, d), F32),
                   jax.ShapeDtypeStruct((n_rows, 128), jnp.int32),
                   jax.ShapeDtypeStruct((n_rows, 128), F32),
                   jax.ShapeDtypeStruct((n_rows, 128), jnp.int32),
                   jax.ShapeDtypeStruct((1, n_exp), jnp.int32)],
        scratch_shapes=[pltpu.VMEM((1, n_exp), F32)],
        compiler_params=_cparams("arbitrary"),
        name="router",
    )(x_all, mods, mods, g2, router_w, router_b.reshape(1, n_exp), tri)


def _dispatch_kernel(fill_lo_ref, fill_hi_ref, dest_ref, h_ref, xs_ref, sem, *, tg, n_exp):
    i = pl.program_id(0)

    def row_copy(t, slot):
        return pltpu.make_async_copy(h_ref.at[pl.ds(t, 1)], xs_ref.at[pl.ds(slot, 1)], sem)

    def issue(t, c):
        for k in range(TOP_K):
            row_copy(t, dest_ref[t * TOP_K + k]).start()
        return c

    lax.fori_loop(0, tg, issue, 0)

    def drain(t, c):
        for k in range(TOP_K):
            row_copy(0, 0).wait()
        return c

    lax.fori_loop(0, tg, drain, 0)

    @pl.when(i == pl.num_programs(0) - 1)
    def _():
        for e in range(n_exp):
            lo = fill_lo_ref[e]
            hi = fill_hi_ref[e]

            def fill(p, c):
                row_copy(0, p).start()
                return c

            lax.fori_loop(lo, hi, fill, 0)

            def fill_wait(p, c):
                row_copy(0, 0).wait()
                return c

            lax.fori_loop(lo, hi, fill_wait, 0)

        def tail_copy(blk):
            return pltpu.make_async_copy(h_ref, xs_ref.at[pl.ds(pl.multiple_of(blk * tg, tg), tg)], sem)

        first_free = fill_hi_ref[n_exp - 1] // tg

        def tail(blk, c):
            tail_copy(blk).start()
            tail_copy(blk).wait()
            return c

        lax.fori_loop(first_free, xs_ref.shape[0] // tg, tail, 0)


def dispatch(h2, dest_flat, fill_lo, fill_hi, cap):
    n, d = h2.shape
    tg = TG
    assert BM % tg == 0 and cap % tg == 0
    n_exp = fill_lo.shape[0]
    return pl.pallas_call(
        functools.partial(_dispatch_kernel, tg=tg, n_exp=n_exp),
        grid_spec=pltpu.PrefetchScalarGridSpec(
            num_scalar_prefetch=2,
            grid=(n // tg,),
            in_specs=[pl.BlockSpec((tg * TOP_K,), lambda i, lo, hi: (i,), memory_space=pltpu.SMEM),
                      pl.BlockSpec((tg, d), lambda i, lo, hi: (i, 0))],
            out_specs=pl.BlockSpec(memory_space=pl.ANY),
            scratch_shapes=[pltpu.SemaphoreType.DMA(())]),
        out_shape=jax.ShapeDtypeStruct((cap, d), F32),
        compiler_params=_cparams("arbitrary"),
        name="moe_dispatch",
    )(fill_lo, fill_hi, dest_flat, h2)


def _expert_kernel(be_ref, nu_ref, x_ref, wg_ref, wu_ref, bg_ref, bu_ref, wd_ref, bd_ref, y_ref):
    m = pl.program_id(0)

    @pl.when(m < nu_ref[0])
    def _():
        x = x_ref[...].astype(BF16)
        hg = jnp.dot(x, wg_ref[0], preferred_element_type=F32) + bg_ref[0]
        hu = jnp.dot(x, wu_ref[0], preferred_element_type=F32) + bu_ref[0]
        g = jnp.minimum(hg, SWIGLU_LIMIT)
        u = jnp.clip(hu, -SWIGLU_LIMIT, SWIGLU_LIMIT)
        a = (u + 1.0) * (g * jax.nn.sigmoid(SWIGLU_ALPHA * g))
        y_ref[...] = jnp.dot(a.astype(BF16), wd_ref[0], preferred_element_type=F32) + bd_ref[0]

    @pl.when(m >= nu_ref[0])
    def _():
        y_ref[...] = jnp.zeros_like(y_ref)


def experts(xs, block_e, n_used, wg, wu, bg, bu, wd, bd):
    cap, d = xs.shape
    bm = BM
    n_exp, _, f = wg.shape
    n_blocks = cap // bm

    def blk(m, be, nu):
        return jnp.minimum(m, nu[0] - 1)

    def exp_of(m, be, nu):
        return be[blk(m, be, nu)]

    return pl.pallas_call(
        _expert_kernel,
        grid_spec=pltpu.PrefetchScalarGridSpec(
            num_scalar_prefetch=2,
            grid=(n_blocks,),
            in_specs=[pl.BlockSpec((bm, d), lambda m, be, nu: (blk(m, be, nu), 0)),
                      pl.BlockSpec((1, d, f), lambda m, be, nu: (exp_of(m, be, nu), 0, 0)),
                      pl.BlockSpec((1, d, f), lambda m, be, nu: (exp_of(m, be, nu), 0, 0)),
                      pl.BlockSpec((1, 1, f), lambda m, be, nu: (exp_of(m, be, nu), 0, 0)),
                      pl.BlockSpec((1, 1, f), lambda m, be, nu: (exp_of(m, be, nu), 0, 0)),
                      pl.BlockSpec((1, f, d), lambda m, be, nu: (exp_of(m, be, nu), 0, 0)),
                      pl.BlockSpec((1, 1, d), lambda m, be, nu: (exp_of(m, be, nu), 0, 0))],
            out_specs=pl.BlockSpec((bm, d), lambda m, be, nu: (m, 0))),
        out_shape=jax.ShapeDtypeStruct((cap, d), F32),
        compiler_params=_cparams("arbitrary"),
        name="moe_experts",
    )(block_e, n_used, xs, wg, wu, bg, bu, wd, bd)


def _combine_kernel(dest_ref, y_ref, x_ref, gt_ref, gate_ref, fg_ref, o_ref, buf, sem, *, tg, final_norm):
    def row_copy(slot, k, t):
        return pltpu.make_async_copy(y_ref.at[pl.ds(slot, 1)], buf.at[k, pl.ds(t, 1)], sem)

    def issue(t, c):
        for k in range(TOP_K):
            row_copy(dest_ref[t * TOP_K + k], k, t).start()
        return c

    lax.fori_loop(0, tg, issue, 0)

    def drain(t, c):
        for k in range(TOP_K):
            row_copy(0, 0, 0).wait()
        return c

    lax.fori_loop(0, tg, drain, 0)

    acc = gate_ref[:, 0:1] * buf[0]
    for k in range(1, TOP_K):
        acc += gate_ref[:, k:k + 1] * buf[k]
    xn = x_ref[...] + gt_ref[0] * acc
    if final_norm:
        xn = _rms(xn) * fg_ref[...]
    o_ref[...] = xn


def combine(x_all, mods, y, dest_flat, gates, final_g, *, n_rows, seq, n_batch, final_norm):
    d = x_all.shape[1]
    tg = TG

    def mod_row(i):
        return jnp.minimum((i * tg) // seq, n_batch)

    return pl.pallas_call(
        functools.partial(_combine_kernel, tg=tg, final_norm=final_norm),
        grid=(n_rows // tg,),
        in_specs=[pl.BlockSpec((tg * TOP_K,), lambda i: (i,), memory_space=pltpu.SMEM),
                  pl.BlockSpec(memory_space=pl.ANY),
                  pl.BlockSpec((tg, d), lambda i: (i, 0)),
                  pl.BlockSpec((1, 1, d), lambda i: (mod_row(i), 0, 5)),
                  pl.BlockSpec((tg, 128), lambda i: (i, 0)),
                  pl.BlockSpec((1, d), lambda i: (0, 0))],
        out_specs=pl.BlockSpec((tg, d), lambda i: (i, 0)),
        out_shape=jax.ShapeDtypeStruct((n_rows, d), F32),
        scratch_shapes=[pltpu.VMEM((TOP_K, tg, d), F32), pltpu.SemaphoreType.DMA(())],
        compiler_params=_cparams("arbitrary"),
        name="moe_combine",
    )(dest_flat, y, x_all, mods, gates, final_g)


def _rope_tables(seq, tm):
    half = HEAD_DIM // 2
    inv = ROPE_THETA ** (-jnp.arange(0, half, 2, dtype=F32) / half)
    t = jnp.arange(seq)
    lane = np.arange(128)
    dd = lane % HEAD_DIM
    use_col = (dd // half) == 1
    j = dd % (half // 2)
    second = (dd % half) >= (half // 2)
    pos = jnp.where(jnp.asarray(use_col)[None, :], (t % GRID_W)[:, None], (t // GRID_W)[:, None]).astype(F32)
    ang = pos * inv[jnp.asarray(j)][None, :]
    cos = jnp.cos(ang)
    sin = jnp.sin(ang) * jnp.where(jnp.asarray(second), 1.0, -1.0)[None, :]
    cos = jnp.concatenate([cos, jnp.ones((tm, 128), F32)], axis=0)
    sin = jnp.concatenate([sin, jnp.zeros((tm, 128), F32)], axis=0)
    return cos, sin


def _moe_plan(idx, rank, counts, n_exp, n_rows):
    bm = BM
    nk = n_rows * TOP_K
    n_blocks = -(-nk // bm) + n_exp
    counts = counts.reshape(n_exp)
    padded = (counts + bm - 1) // bm * bm
    pend = jnp.cumsum(padded)
    pstart = pend - padded
    dest = (pstart[idx] + rank).reshape(nk).astype(jnp.int32)
    block_e = jnp.minimum(jnp.searchsorted(pend, jnp.arange(n_blocks) * bm, side='right'),
                          n_exp - 1).astype(jnp.int32)
    n_used = (pend[-1:] // bm).astype(jnp.int32)
    return dest, block_e, n_used, (pstart + counts).astype(jnp.int32), pend.astype(jnp.int32), n_blocks * bm


def kernel(x, c, ctx, c_ctx, ada_w, ada_b, norm1_g, norm2_g, w_in, conv_w, na_rpb, q_norm_g, k_norm_g,
           w_br_conv, w_br_na, w_br_gqa, w_out, router_w, router_b, w_gu, b_gu, w_down, b_down, final_g):
    n_batch, seq, d = x.shape
    n_ctx = ctx.shape[1]
    depth = ada_w.shape[0]
    n_exp = router_w.shape[-1]
    n_lat = n_batch * seq
    n_all = n_lat + n_batch * n_ctx
    assert n_batch + 1 <= MOD_ROWS and seq % TM == 0 and (n_batch * n_ctx) % TM == 0

    cond = jnp.zeros((MOD_ROWS, d), F32).at[:n_batch].set(c).at[n_batch].set(c_ctx)
    mods_all = ada_mods(cond, ada_w, ada_b)
    cos_t, sin_t = _rope_tables(seq, TM)
    hsum = jnp.asarray(np.kron(np.eye(128 // HEAD_DIM), np.ones((HEAD_DIM, HEAD_DIM))), BF16)

    x_all = jnp.concatenate([x.reshape(n_lat, d), ctx.reshape(n_batch * n_ctx, d)], axis=0)
    for l in range(depth):
        last = l == depth - 1
        mods = mods_all[l].reshape(MOD_ROWS, 1, 6 * d)
        qg = jnp.tile(q_norm_g[l], 128 // HEAD_DIM).reshape(1, 128)
        kg = jnp.tile(k_norm_g[l], 128 // HEAD_DIM).reshape(1, 128)
        conv, naq, nak, nav, gq, gk, gv, gates = in_projection(
            x_all, mods, norm1_g[l].reshape(1, d), w_in[l].astype(BF16), cos_t, sin_t, qg, kg, hsum,
            seq=seq, n_lat=n_lat, n_batch=n_batch)

        y_na = neighborhood_attention(naq, nak, nav, na_bias_table(na_rpb[l], seq),
                                      seq=seq, ctx=n_ctx, n_batch=n_batch)

        def kv_all(t):
            lat = t[:n_lat].reshape(n_batch, seq, GQA_KV_HEADS, HEAD_DIM)
            cx = t[n_lat:].reshape(n_batch, n_ctx, GQA_KV_HEADS, HEAD_DIM)
            return jnp.concatenate([lat, cx], axis=1)

        kt_all = kv_all(gk).transpose(0, 2, 3, 1)
        v_all = kv_all(gv).transpose(0, 2, 1, 3)
        y_gqa = gqa_attention(gq, kt_all, v_all, seq=seq, n_batch=n_batch)

        n_rows = n_lat if last else n_all
        if not last:
            yc_na, yc_gqa = ctx_attention(naq, nak, nav, gq, gk, gv, ctx=n_ctx, n_lat=n_lat, n_batch=n_batch)
            y_na = jnp.concatenate([y_na, yc_na], axis=0)
            y_gqa = jnp.concatenate([y_gqa, yc_gqa], axis=0)
        x_all = merge_branches(x_all, mods, conv, y_na, y_gqa, gates, conv_w[l],
                               w_br_conv[l].astype(BF16), w_br_na[l].astype(BF16), w_br_gqa[l].astype(BF16),
                               w_out[l].astype(BF16), n_rows=n_rows, seq=seq, ctx=n_ctx, n_lat=n_lat,
                               n_batch=n_batch)

        h2, idx, gate_w, rank, counts = router(x_all, mods, norm2_g[l].reshape(1, d), router_w[l], router_b[l],
                                               n_rows=n_rows, seq=seq, n_batch=n_batch)
        dest, block_e, n_used, fill_lo, fill_hi, cap = _moe_plan(
            idx[:, :TOP_K], rank[:, :TOP_K], counts, n_exp, n_rows)
        xs = dispatch(h2, dest, fill_lo, fill_hi, cap)
        f = w_gu.shape[-1] // 2
        y = experts(xs, block_e, n_used,
                    w_gu[l, :, :, 0::2].astype(BF16), w_gu[l, :, :, 1::2].astype(BF16),
                    b_gu[l, :, 0::2].reshape(n_exp, 1, f), b_gu[l, :, 1::2].reshape(n_exp, 1, f),
                    w_down[l].astype(BF16), b_down[l].reshape(n_exp, 1, d))
        x_all = combine(x_all, mods, y, dest, gate_w, final_g.reshape(1, d),
                        n_rows=n_rows, seq=seq, n_batch=n_batch, final_norm=last)
    return x_all.reshape(n_batch, seq, d)
```

```python
import functools

import jax
import jax.numpy as jnp
import numpy as np
from jax import lax
from jax.experimental import pallas as pl
from jax.experimental.pallas import tpu as pltpu

F32 = jnp.float32
BF16 = jnp.bfloat16

GRID_W = 64
EPS = 1e-6
HEAD_DIM = 64
CONV_WIDTH = 512
NA_HEADS = 8
NA_WIDTH = NA_HEADS * HEAD_DIM
NA_KH = 8
NA_KW = 16
NA_QROWS = 4
NA_BAND = NA_QROWS + NA_KH
GQA_Q_HEADS = 8
GQA_KV_HEADS = 2
GQA_GROUP = GQA_Q_HEADS // GQA_KV_HEADS
GQA_WIDTH = GQA_Q_HEADS * HEAD_DIM
GQA_KV_WIDTH = GQA_KV_HEADS * HEAD_DIM
ROPE_THETA = 10000.0
TOP_K = 4
SWIGLU_ALPHA = 1.702
SWIGLU_LIMIT = 7.0
GQA_Q_SCALE = HEAD_DIM ** -0.5 * float(np.log2(np.e))
MOD_ROWS = 16
NEG_BIG = -1e30
VMEM_LIMIT = 56 * 1024 * 1024

TM = 512
TQ = 128
GQA_KEY_CHUNK = 1024
BM = 256
TG = 256


def _cparams(*sem):
    return pltpu.CompilerParams(dimension_semantics=sem, vmem_limit_bytes=VMEM_LIMIT)


def _rms(x):
    return x * lax.rsqrt(jnp.mean(x * x, axis=-1, keepdims=True) + EPS)


def _ada_kernel(c_ref, w_ref, b_ref, o_ref):
    c = c_ref[...]
    s = c * jax.nn.sigmoid(c)
    o_ref[0] = jnp.dot(s, w_ref[0], precision=lax.Precision.HIGHEST,
                       preferred_element_type=F32) + b_ref[0]


def ada_mods(cond, ada_w, ada_b):
    depth, d, n6 = ada_w.shape
    tn = d
    return pl.pallas_call(
        _ada_kernel,
        grid=(depth, n6 // tn),
        in_specs=[pl.BlockSpec((MOD_ROWS, d), lambda l, j: (0, 0)),
                  pl.BlockSpec((1, d, tn), lambda l, j: (l, 0, j)),
                  pl.BlockSpec((1, 1, tn), lambda l, j: (l, 0, j))],
        out_specs=pl.BlockSpec((1, MOD_ROWS, tn), lambda l, j: (l, 0, j)),
        out_shape=jax.ShapeDtypeStruct((depth, MOD_ROWS, n6), F32),
        compiler_params=_cparams("arbitrary", "arbitrary"),
        name="ada_mods",
    )(cond, ada_w, ada_b.reshape(depth, 1, n6))


def _head_inv_rms(t, hsum_ref):
    s = t * t
    s_hi = s.astype(BF16)
    s_lo = (s - s_hi.astype(F32)).astype(BF16)
    tot = (jnp.dot(s_hi, hsum_ref[...], preferred_element_type=F32)
           + jnp.dot(s_lo, hsum_ref[...], preferred_element_type=F32))
    return lax.rsqrt(tot * (1.0 / HEAD_DIM) + EPS)


def _rope(t, cos, sin):
    lane = lax.broadcasted_iota(jnp.int32, t.shape, 1)
    second = (lane % 32) >= 16
    partner = jnp.where(second, pltpu.roll(t, 16, 1), pltpu.roll(t, 128 - 16, 1))
    return t * cos + partner * sin


def _inproj_kernel(x_ref, sh_ref, sc_ref, g_ref, w_ref, cos_ref, sin_ref, qg_ref, kg_ref, hsum_ref,
                   conv_o, naq_o, nak_o, nav_o, gq_o, gk_o, gv_o, gate_o, h_scr, *, d):
    h = _rms(x_ref[...]) * g_ref[...]
    h = h * (1.0 + sc_ref[0]) + sh_ref[0]
    h_scr[...] = h.astype(BF16)

    def proj(a, n):
        return jnp.dot(h_scr[...], w_ref[:, a:a + n], preferred_element_type=F32)

    off = 0
    for j in range(3):
        conv_o[:, j * CONV_WIDTH:(j + 1) * CONV_WIDTH] = proj(off, CONV_WIDTH).astype(BF16)
        off += CONV_WIDTH
    naq_o[...] = (proj(off, NA_WIDTH) * (HEAD_DIM ** -0.5)).astype(BF16)
    off += NA_WIDTH
    nak_o[...] = proj(off, NA_WIDTH).astype(BF16)
    off += NA_WIDTH
    nav_o[...] = proj(off, NA_WIDTH).astype(BF16)
    off += NA_WIDTH
    cos = cos_ref[...]
    sin = sin_ref[...]
    for j in range(GQA_WIDTH // 128):
        t = proj(off + j * 128, 128)
        t = t * _head_inv_rms(t, hsum_ref) * qg_ref[...]
        gq_o[:, j * 128:(j + 1) * 128] = (_rope(t, cos, sin) * GQA_Q_SCALE).astype(BF16)
    off += GQA_WIDTH
    t = proj(off, GQA_KV_WIDTH)
    t = t * _head_inv_rms(t, hsum_ref) * kg_ref[...]
    gk_o[...] = _rope(t, cos, sin).astype(BF16)
    off += GQA_KV_WIDTH
    gv_o[...] = proj(off, GQA_KV_WIDTH).astype(BF16)
    off += GQA_KV_WIDTH
    for j in range(3):
        gate_o[:, j * d:(j + 1) * d] = jax.nn.sigmoid(proj(off + j * d, d)).astype(BF16)


def in_projection(x_all, mods, g1, w_in_bf, cos_t, sin_t, qg, kg, hsum, *, seq, n_lat, n_batch):
    nt, d = x_all.shape
    tm = TM
    n_lat_tiles = n_lat // tm
    seq_tiles = seq // tm
    in_cols = w_in_bf.shape[1]

    def mod_row(i):
        return jnp.minimum((i * tm) // seq, n_batch)

    def rope_blk(i):
        return jnp.where(i < n_lat_tiles, i % seq_tiles, seq_tiles)

    tok = lambda w: pl.BlockSpec((tm, w), lambda i: (i, 0))
    const = lambda shape: pl.BlockSpec(shape, lambda i: tuple(0 for _ in shape))
    outs = [CONV_WIDTH * 3, NA_WIDTH, NA_WIDTH, NA_WIDTH, GQA_WIDTH, GQA_KV_WIDTH, GQA_KV_WIDTH, 3 * d]
    return pl.pallas_call(
        functools.partial(_inproj_kernel, d=d),
        grid=(nt // tm,),
        in_specs=[tok(d),
                  pl.BlockSpec((1, 1, d), lambda i: (mod_row(i), 0, 0)),
                  pl.BlockSpec((1, 1, d), lambda i: (mod_row(i), 0, 1)),
                  const((1, d)),
                  pl.BlockSpec((d, in_cols), lambda i: (0, 0), pipeline_mode=pl.Buffered(1)),
                  pl.BlockSpec((tm, 128), lambda i: (rope_blk(i), 0)),
                  pl.BlockSpec((tm, 128), lambda i: (rope_blk(i), 0)),
                  const((1, 128)), const((1, 128)), const((128, 128))],
        out_specs=[tok(w) for w in outs],
        out_shape=[jax.ShapeDtypeStruct((nt, w), BF16) for w in outs],
        scratch_shapes=[pltpu.VMEM((tm, d), BF16)],
        compiler_params=_cparams("arbitrary"),
        name="in_projection",
    )(x_all, mods, mods, g1, w_in_bf, cos_t, sin_t, qg, kg, hsum)


def _softmax_pv(scores, values, exp=jnp.exp):
    m = functools.reduce(jnp.maximum, [jnp.max(s, axis=-1, keepdims=True) for s in scores])
    ps = [exp(s - m) for s in scores]
    l = functools.reduce(lambda a, b: a + b, [jnp.sum(p, axis=-1, keepdims=True) for p in ps])
    o = functools.reduce(lambda a, b: a + b,
                         [jnp.dot(p.astype(BF16), v, preferred_element_type=F32) for p, v in zip(ps, values)])
    return o / l


def _qk(q, k):
    return lax.dot_general(q, k, (((1,), (1,)), ((), ())), preferred_element_type=F32)


def _na_kernel(q_ref, k_ref, v_ref, kc_ref, vc_ref, bias_ref, o_ref, *, w, rows):
    j = pl.program_id(1)
    start = jnp.clip(j * NA_QROWS - NA_KH // 2, 0, rows - NA_BAND) * w
    start = pl.multiple_of(start, w)
    nkeys = NA_BAND * w
    for h in range(NA_HEADS):
        hs = slice(h * HEAD_DIM, (h + 1) * HEAD_DIM)
        q = q_ref[:, hs]
        kw = k_ref[pl.ds(start, nkeys), hs]
        vw = v_ref[pl.ds(start, nkeys), hs]
        s_win = _qk(q, kw) + bias_ref[0, h]
        s_ctx = _qk(q, kc_ref[:, hs])
        o_ref[:, hs] = _softmax_pv([s_win, s_ctx], [vw, vc_ref[:, hs]]).astype(BF16)


def neighborhood_attention(naq, nak, nav, bias, *, seq, ctx, n_batch):
    w = GRID_W
    rows = seq // w
    nj = rows // NA_QROWS
    nq = NA_QROWS * w
    n_lat = n_batch * seq
    ctx_blk0 = n_lat // ctx

    def case(j):
        return jnp.where(j == 0, 0, jnp.where(j == nj - 1, 2, 1))

    return pl.pallas_call(
        functools.partial(_na_kernel, w=w, rows=rows),
        grid=(n_batch, nj),
        in_specs=[pl.BlockSpec((nq, NA_WIDTH), lambda b, j: (b * nj + j, 0)),
                  pl.BlockSpec((seq, NA_WIDTH), lambda b, j: (b, 0)),
                  pl.BlockSpec((seq, NA_WIDTH), lambda b, j: (b, 0)),
                  pl.BlockSpec((ctx, NA_WIDTH), lambda b, j: (ctx_blk0 + b, 0)),
                  pl.BlockSpec((ctx, NA_WIDTH), lambda b, j: (ctx_blk0 + b, 0)),
                  pl.BlockSpec((1, NA_HEADS, nq, NA_BAND * w), lambda b, j: (case(j), 0, 0, 0))],
        out_specs=pl.BlockSpec((nq, NA_WIDTH), lambda b, j: (b * nj + j, 0)),
        out_shape=jax.ShapeDtypeStruct((n_lat, NA_WIDTH), BF16),
        compiler_params=_cparams("arbitrary", "arbitrary"),
        name="neighborhood_attention",
    )(naq, nak, nav, nak, nav, bias)


def na_bias_table(rpb, seq):
    w = GRID_W
    rows = seq // w
    nj = rows // NA_QROWS
    n_heads = rpb.shape[0]
    cols = np.arange(w)
    col_start = np.clip(cols - NA_KW // 2, 0, w - NA_KW)
    col_ok = (cols[None, :] >= col_start[:, None]) & (cols[None, :] < col_start[:, None] + NA_KW)
    pad = max(0, w - NA_KW)
    rp = jnp.pad(rpb.astype(F32), ((0, 0), (0, 0), (pad, pad)))
    tcol = jnp.stack([rp[:, :, pad + NA_KW - 1 - c: pad + NA_KW - 1 - c + w] for c in range(w)], axis=2)
    tcol = jnp.where(jnp.asarray(col_ok)[None, None], tcol, NEG_BIG)
    masked = jnp.full((n_heads, w, w), NEG_BIG, F32)
    tabs = []
    for j in (0, 1, nj - 1):
        start = int(np.clip(j * NA_QROWS - NA_KH // 2, 0, rows - NA_BAND))
        q_rows = []
        for a in range(NA_QROWS):
            r = j * NA_QROWS + a
            rs = int(np.clip(r - NA_KH // 2, 0, rows - NA_KH))
            blocks = []
            for i in range(NA_BAND):
                kr = start + i
                blocks.append(tcol[:, kr - r + NA_KH - 1] if rs <= kr < rs + NA_KH else masked)
            q_rows.append(jnp.concatenate(blocks, axis=-1))
        tabs.append(jnp.concatenate(q_rows, axis=1))
    return jnp.stack(tabs)


def _online_softmax_pv(q, kt_ref, v_ref, chunks):
    m = jnp.full((q.shape[0], 1), -jnp.inf, F32)
    acc = jnp.zeros((q.shape[0], v_ref.shape[-1]), F32)
    for a, n in chunks:
        s = jnp.dot(q, kt_ref[:, a:a + n], preferred_element_type=F32)
        m_new = jnp.maximum(m, jnp.max(s, axis=-1, keepdims=True))
        p = jnp.exp2(s - m_new).astype(BF16)
        acc = acc * jnp.exp2(m - m_new) + jnp.dot(p, v_ref[a:a + n, :], preferred_element_type=F32)
        m = m_new
    return acc[:, :HEAD_DIM] / acc[:, HEAD_DIM:HEAD_DIM + 1]


def _gqa_kernel(q_ref, kt_ref, v_ref, o_ref, *, tq, chunks):
    q = jnp.concatenate([q_ref[:, g * HEAD_DIM:(g + 1) * HEAD_DIM] for g in range(GQA_GROUP)], axis=0)
    o = _online_softmax_pv(q, kt_ref.at[0, 0], v_ref.at[0, 0], chunks)
    for g in range(GQA_GROUP):
        o_ref[:, g * HEAD_DIM:(g + 1) * HEAD_DIM] = o[g * tq:(g + 1) * tq].astype(BF16)


def _key_chunks(total, size):
    return tuple((a, min(size, total - a)) for a in range(0, total, size))


def gqa_attention(gq, kt_all, v_all, *, seq, n_batch):
    tq = TQ
    nq = seq // tq
    s_all = kt_all.shape[-1]
    vw = v_all.shape[-1]
    gw = GQA_GROUP * HEAD_DIM
    return pl.pallas_call(
        functools.partial(_gqa_kernel, tq=tq, chunks=_key_chunks(s_all, GQA_KEY_CHUNK)),
        grid=(n_batch, GQA_KV_HEADS, nq),
        in_specs=[pl.BlockSpec((tq, gw), lambda b, kv, i: (b * nq + i, kv)),
                  pl.BlockSpec((1, 1, HEAD_DIM, s_all), lambda b, kv, i: (b, kv, 0, 0)),
                  pl.BlockSpec((1, 1, s_all, vw), lambda b, kv, i: (b, kv, 0, 0))],
        out_specs=pl.BlockSpec((tq, gw), lambda b, kv, i: (b * nq + i, kv)),
        out_shape=jax.ShapeDtypeStruct((n_batch * seq, GQA_WIDTH), BF16),
        compiler_params=_cparams("arbitrary", "arbitrary", "arbitrary"),
        name="gqa_attention",
    )(gq, kt_all, v_all)


def _ctx_attn_kernel(naq_ref, nak_ref, nav_ref, gq_ref, gk_ref, gv_ref, ona_ref, ogqa_ref):
    for h in range(NA_HEADS):
        hs = slice(h * HEAD_DIM, (h + 1) * HEAD_DIM)
        ona_ref[:, hs] = _softmax_pv([_qk(naq_ref[:, hs], nak_ref[:, hs])], [nav_ref[:, hs]]).astype(BF16)
    for h in range(GQA_Q_HEADS):
        hs = slice(h * HEAD_DIM, (h + 1) * HEAD_DIM)
        kv = h // GQA_GROUP
        ks = slice(kv * HEAD_DIM, (kv + 1) * HEAD_DIM)
        ogqa_ref[:, hs] = _softmax_pv([_qk(gq_ref[:, hs], gk_ref[:, ks])], [gv_ref[:, ks]],
                                      exp=jnp.exp2).astype(BF16)


def ctx_attention(naq, nak, nav, gq, gk, gv, *, ctx, n_lat, n_batch):
    blk0 = n_lat // ctx
    wide = lambda: pl.BlockSpec((ctx, NA_WIDTH), lambda b: (blk0 + b, 0))
    narrow = lambda: pl.BlockSpec((ctx, GQA_KV_WIDTH), lambda b: (blk0 + b, 0))
    out = lambda: pl.BlockSpec((ctx, NA_WIDTH), lambda b: (b, 0))
    return pl.pallas_call(
        _ctx_attn_kernel,
        grid=(n_batch,),
        in_specs=[wide(), wide(), wide(), wide(), narrow(), narrow()],
        out_specs=[out(), out()],
        out_shape=[jax.ShapeDtypeStruct((n_batch * ctx, NA_WIDTH), BF16)] * 2,
        compiler_params=_cparams("arbitrary"),
        name="ctx_attention",
    )(naq, nak, nav, gq, gk, gv)


def _merge_kernel(x_ref, gt_ref, conv_ref, prev_ref, next_ref, yna_ref, ygqa_ref, gate_ref,
                  cw_ref, wc_ref, wn_ref, wg_ref, wo_ref, o_ref, *, tm, d, seq, ctx, n_lat):
    i = pl.program_id(0)
    cw = CONV_WIDTH

    def u_of(ref, rows):
        return ref[rows, cw:2 * cw].astype(F32) * ref[rows, 2 * cw:3 * cw].astype(F32)

    u = u_of(conv_ref, slice(None))
    hp = u_of(prev_ref, slice(15, 16))
    hn = u_of(next_ref, slice(0, 1))
    row = lax.broadcasted_iota(jnp.int32, (tm, 1), 0)
    r = i * tm + row
    pos = jnp.where(r < n_lat, r % seq, r % ctx)
    length = jnp.where(r < n_lat, seq, ctx)
    u_prev = jnp.where(row == 0, hp, pltpu.roll(u, 1, 0))
    u_prev = jnp.where(pos == 0, 0.0, u_prev)
    u_next = jnp.where(row == tm - 1, hn, pltpu.roll(u, tm - 1, 0))
    u_next = jnp.where(pos == length - 1, 0.0, u_next)
    y_conv = conv_ref[:, 0:cw].astype(F32) * (u_prev * cw_ref[0:1] + u * cw_ref[1:2] + u_next * cw_ref[2:3])

    m = gate_ref[:, 0:d].astype(F32) * jnp.dot(y_conv.astype(BF16), wc_ref[...], preferred_element_type=F32)
    m += gate_ref[:, d:2 * d].astype(F32) * jnp.dot(yna_ref[...], wn_ref[...], preferred_element_type=F32)
    m += gate_ref[:, 2 * d:3 * d].astype(F32) * jnp.dot(ygqa_ref[...], wg_ref[...], preferred_element_type=F32)
    out = jnp.dot(m.astype(BF16), wo_ref[...], preferred_element_type=F32)
    o_ref[...] = x_ref[...] + gt_ref[0] * out


def merge_branches(x_all, mods, conv, y_na, y_gqa, gates, conv_w, wc, wn, wg, wo, *, n_rows, seq, ctx,
                   n_lat, n_batch):
    d = x_all.shape[1]
    tm = TM
    halo = 16
    n_halo_blocks = conv.shape[0] // halo
    per = tm // halo

    def mod_row(i):
        return jnp.minimum((i * tm) // seq, n_batch)

    tok = lambda w: pl.BlockSpec((tm, w), lambda i: (i, 0))
    const = lambda shape: pl.BlockSpec(shape, lambda i: tuple(0 for _ in shape))
    return pl.pallas_call(
        functools.partial(_merge_kernel, tm=tm, d=d, seq=seq, ctx=ctx, n_lat=n_lat),
        grid=(n_rows // tm,),
        in_specs=[tok(d),
                  pl.BlockSpec((1, 1, d), lambda i: (mod_row(i), 0, 2)),
                  tok(3 * CONV_WIDTH),
                  pl.BlockSpec((halo, 3 * CONV_WIDTH), lambda i: (jnp.maximum(i * per - 1, 0), 0)),
                  pl.BlockSpec((halo, 3 * CONV_WIDTH),
                               lambda i: (jnp.minimum((i + 1) * per, n_halo_blocks - 1), 0)),
                  tok(NA_WIDTH), tok(GQA_WIDTH), tok(3 * d),
                  const((3, CONV_WIDTH)), const((CONV_WIDTH, d)), const((NA_WIDTH, d)),
                  const((GQA_WIDTH, d)), const((d, d))],
        out_specs=tok(d),
        out_shape=jax.ShapeDtypeStruct((n_rows, d), F32),
        compiler_params=_cparams("arbitrary"),
        name="merge_branches",
    )(x_all, mods, conv, conv, conv, y_na, y_gqa, gates, conv_w, wc, wn, wg, wo)


def _router_kernel(x_ref, sh_ref, sc_ref, g_ref, rw_ref, rb_ref, tri_ref,
                   h_o, idx_o, gate_o, rank_o, cnt_o, carry, *, n_exp):
    i = pl.program_id(0)

    @pl.when(i == 0)
    def _():
        carry[...] = jnp.zeros_like(carry)

    h = _rms(x_ref[...]) * g_ref[...]
    h = h * (1.0 + sc_ref[0]) + sh_ref[0]
    h_o[...] = h
    logits = jnp.dot(h, rw_ref[...], precision=lax.Precision.HIGHEST,
                     preferred_element_type=F32) + rb_ref[...]
    lane = lax.broadcasted_iota(jnp.int32, logits.shape, 1)
    work = logits
    vals, idxs, hots = [], [], []
    for _ in range(TOP_K):
        m = jnp.max(work, axis=-1, keepdims=True)
        ik = jnp.min(jnp.where(work == m, lane, n_exp), axis=-1, keepdims=True)
        hot = lane == ik
        vals.append(m)
        idxs.append(ik)
        hots.append(hot)
        work = jnp.where(hot, -jnp.inf, work)
    es = [jnp.exp(v - vals[0]) for v in vals]
    den = functools.reduce(lambda a, b: a + b, es)
    hot_sum = functools.reduce(lambda a, b: a + b, [jnp.where(hh, 1.0, 0.0) for hh in hots])
    prefix = jnp.dot(tri_ref[...], hot_sum.astype(BF16), preferred_element_type=F32) + carry[...]
    ranks = [jnp.sum(jnp.where(hh, prefix, 0.0), axis=-1, keepdims=True) for hh in hots]
    carry[...] = carry[...] + jnp.sum(hot_sum, axis=0, keepdims=True)
    cnt_o[...] = carry[...].astype(jnp.int32)

    out_lane = lax.broadcasted_iota(jnp.int32, idx_o.shape, 1)

    def spread(cols, fill):
        acc = jnp.full(idx_o.shape, fill, cols[0].dtype)
        for k, c in enumerate(cols):
            acc = jnp.where(out_lane == k, c, acc)
        return acc

    idx_o[...] = spread(idxs, 0)
    gate_o[...] = spread([e / den for e in es], 0.0)
    rank_o[...] = spread([rk.astype(jnp.int32) for rk in ranks], 0)


def router(x_all, mods, g2, router_w, router_b, *, n_rows, seq, n_batch):
    d = x_all.shape[1]
    n_exp = router_w.shape[1]
    tm = TM
    tri = jnp.asarray(np.tril(np.ones((tm, tm), np.float32), -1), BF16)

    def mod_row(i):
        return jnp.minimum((i * tm) // seq, n_batch)

    tok = lambda w: pl.BlockSpec((tm, w), lambda i: (i, 0))
    const = lambda shape: pl.BlockSpec(shape, lambda i: tuple(0 for _ in shape))
    return pl.pallas_call(
        functools.partial(_router_kernel, n_exp=n_exp),
        grid=(n_rows // tm,),
        in_specs=[tok(d),
                  pl.BlockSpec((1, 1, d), lambda i: (mod_row(i), 0, 3)),
                  pl.BlockSpec((1, 1, d), lambda i: (mod_row(i), 0, 4)),
                  const((1, d)), const((d, n_exp)), const((1, n_exp)), const((tm, tm))],
        out_specs=[tok(d), tok(128), tok(128), tok(128), const((1, n_exp))],
        out_shape=[jax.ShapeDtypeStruct((n_rows, d), F32),
                   jax.ShapeDtypeStruct((n_rows, 128), jnp.int32),
                   jax.ShapeDtypeStruct((n_rows, 128), F32),
                   jax.ShapeDtypeStruct((n_rows, 128), jnp.int32),
                   jax.ShapeDtypeStruct((1, n_exp), jnp.int32)],
        scratch_shapes=[pltpu.VMEM((1, n_exp), F32)],
        compiler_params=_cparams("arbitrary"),
        name="router",
    )(x_all, mods, mods, g2, router_w, router_b.reshape(1, n_exp), tri)


def _dispatch_kernel(fill_lo_ref, fill_hi_ref, dest_ref, h_ref, xs_ref, sem, *, tg, n_exp):
    i = pl.program_id(0)

    def row_copy(t, slot):
        return pltpu.make_async_copy(h_ref.at[pl.ds(t, 1)], xs_ref.at[pl.ds(slot, 1)], sem)

    def issue(t, c):
        for k in range(TOP_K):
            row_copy(t, dest_ref[t * TOP_K + k]).start()
        return c

    lax.fori_loop(0, tg, issue, 0)

    def drain(t, c):
        for k in range(TOP_K):
            row_copy(0, 0).wait()
        return c

    lax.fori_loop(0, tg, drain, 0)

    @pl.when(i == pl.num_programs(0) - 1)
    def _():
        for e in range(n_exp):
            lo = fill_lo_ref[e]
            hi = fill_hi_ref[e]

            def fill(p, c):
                row_copy(0, p).start()
                return c

            lax.fori_loop(lo, hi, fill, 0)

            def fill_wait(p, c):
                row_copy(0, 0).wait()
                return c

            lax.fori_loop(lo, hi, fill_wait, 0)

        def tail_copy(blk):
            return pltpu.make_async_copy(h_ref, xs_ref.at[pl.ds(pl.multiple_of(blk * tg, tg), tg)], sem)

        first_free = fill_hi_ref[n_exp - 1] // tg

        def tail(blk, c):
            tail_copy(blk).start()
            tail_copy(blk).wait()
            return c

        lax.fori_loop(first_free, xs_ref.shape[0] // tg, tail, 0)


def dispatch(h2, dest_flat, fill_lo, fill_hi, cap):
    n, d = h2.shape
    tg = TG
    assert BM % tg == 0 and cap % tg == 0
    n_exp = fill_lo.shape[0]
    return pl.pallas_call(
        functools.partial(_dispatch_kernel, tg=tg, n_exp=n_exp),
        grid_spec=pltpu.PrefetchScalarGridSpec(
            num_scalar_prefetch=2,
            grid=(n // tg,),
            in_specs=[pl.BlockSpec((tg * TOP_K,), lambda i, lo, hi: (i,), memory_space=pltpu.SMEM),
                      pl.BlockSpec((tg, d), lambda i, lo, hi: (i, 0))],
            out_specs=pl.BlockSpec(memory_space=pl.ANY),
            scratch_shapes=[pltpu.SemaphoreType.DMA(())]),
        out_shape=jax.ShapeDtypeStruct((cap, d), F32),
        compiler_params=_cparams("arbitrary"),
        name="moe_dispatch",
    )(fill_lo, fill_hi, dest_flat, h2)


def _deinterleave_kernel(w_ref, sel_ref, g_ref, u_ref):
    for c in range(w_ref.shape[-1] // 256):
        t = jnp.dot(w_ref[0, :, c * 256:(c + 1) * 256].astype(BF16), sel_ref[...], preferred_element_type=F32)
        g_ref[0, :, c * 128:(c + 1) * 128] = t[:, :128].astype(BF16)
        u_ref[0, :, c * 128:(c + 1) * 128] = t[:, 128:].astype(BF16)


def split_gate_up(w_gu):
    n, d, f2 = w_gu.shape
    sel = np.zeros((256, 256), np.float32)
    sel[2 * np.arange(128), np.arange(128)] = 1.0
    sel[2 * np.arange(128) + 1, 128 + np.arange(128)] = 1.0
    out = pl.BlockSpec((1, d, f2 // 2), lambda e: (e, 0, 0))
    return pl.pallas_call(
        _deinterleave_kernel,
        grid=(n,),
        in_specs=[pl.BlockSpec((1, d, f2), lambda e: (e, 0, 0)),
                  pl.BlockSpec((256, 256), lambda e: (0, 0))],
        out_specs=[out, out],
        out_shape=[jax.ShapeDtypeStruct((n, d, f2 // 2), BF16)] * 2,
        compiler_params=_cparams("arbitrary"),
        name="split_gate_up",
    )(w_gu, jnp.asarray(sel, BF16))


def _expert_kernel(be_ref, nu_ref, x_ref, wg_ref, wu_ref, bg_ref, bu_ref, wd_ref, bd_ref, y_ref):
    m = pl.program_id(0)

    @pl.when(m < nu_ref[0])
    def _():
        x = x_ref[...].astype(BF16)
        hg = jnp.dot(x, wg_ref[0], preferred_element_type=F32) + bg_ref[0]
        hu = jnp.dot(x, wu_ref[0], preferred_element_type=F32) + bu_ref[0]
        g = jnp.minimum(hg, SWIGLU_LIMIT)
        u = jnp.clip(hu, -SWIGLU_LIMIT, SWIGLU_LIMIT)
        a = (u + 1.0) * (g * jax.nn.sigmoid(SWIGLU_ALPHA * g))
        y_ref[...] = jnp.dot(a.astype(BF16), wd_ref[0], preferred_element_type=F32) + bd_ref[0]

    @pl.when(m >= nu_ref[0])
    def _():
        y_ref[...] = jnp.zeros_like(y_ref)


def experts(xs, block_e, n_used, wg, wu, bg, bu, wd, bd):
    cap, d = xs.shape
    bm = BM
    n_exp, _, f = wg.shape
    n_blocks = cap // bm

    def blk(m, be, nu):
        return jnp.minimum(m, nu[0] - 1)

    def exp_of(m, be, nu):
        return be[blk(m, be, nu)]

    return pl.pallas_call(
        _expert_kernel,
        grid_spec=pltpu.PrefetchScalarGridSpec(
            num_scalar_prefetch=2,
            grid=(n_blocks,),
            in_specs=[pl.BlockSpec((bm, d), lambda m, be, nu: (blk(m, be, nu), 0)),
                      pl.BlockSpec((1, d, f), lambda m, be, nu: (exp_of(m, be, nu), 0, 0)),
                      pl.BlockSpec((1, d, f), lambda m, be, nu: (exp_of(m, be, nu), 0, 0)),
                      pl.BlockSpec((1, 1, f), lambda m, be, nu: (exp_of(m, be, nu), 0, 0)),
                      pl.BlockSpec((1, 1, f), lambda m, be, nu: (exp_of(m, be, nu), 0, 0)),
                      pl.BlockSpec((1, f, d), lambda m, be, nu: (exp_of(m, be, nu), 0, 0)),
                      pl.BlockSpec((1, 1, d), lambda m, be, nu: (exp_of(m, be, nu), 0, 0))],
            out_specs=pl.BlockSpec((bm, d), lambda m, be, nu: (m, 0))),
        out_shape=jax.ShapeDtypeStruct((cap, d), F32),
        compiler_params=_cparams("arbitrary"),
        name="moe_experts",
    )(block_e, n_used, xs, wg, wu, bg, bu, wd, bd)


def _combine_kernel(dest_ref, y_ref, x_ref, gt_ref, gate_ref, fg_ref, o_ref, buf, sem, *, tg, final_norm):
    def row_copy(slot, k, t):
        return pltpu.make_async_copy(y_ref.at[pl.ds(slot, 1)], buf.at[k, pl.ds(t, 1)], sem)

    def issue(t, c):
        for k in range(TOP_K):
            row_copy(dest_ref[t * TOP_K + k], k, t).start()
        return c

    lax.fori_loop(0, tg, issue, 0)

    def drain(t, c):
        for k in range(TOP_K):
            row_copy(0, 0, 0).wait()
        return c

    lax.fori_loop(0, tg, drain, 0)

    acc = gate_ref[:, 0:1] * buf[0]
    for k in range(1, TOP_K):
        acc += gate_ref[:, k:k + 1] * buf[k]
    xn = x_ref[...] + gt_ref[0] * acc
    if final_norm:
        xn = _rms(xn) * fg_ref[...]
    o_ref[...] = xn


def combine(x_all, mods, y, dest_flat, gates, final_g, *, n_rows, seq, n_batch, final_norm):
    d = x_all.shape[1]
    tg = TG

    def mod_row(i):
        return jnp.minimum((i * tg) // seq, n_batch)

    return pl.pallas_call(
        functools.partial(_combine_kernel, tg=tg, final_norm=final_norm),
        grid=(n_rows // tg,),
        in_specs=[pl.BlockSpec((tg * TOP_K,), lambda i: (i,), memory_space=pltpu.SMEM),
                  pl.BlockSpec(memory_space=pl.ANY),
                  pl.BlockSpec((tg, d), lambda i: (i, 0)),
                  pl.BlockSpec((1, 1, d), lambda i: (mod_row(i), 0, 5)),
                  pl.BlockSpec((tg, 128), lambda i: (i, 0)),
                  pl.BlockSpec((1, d), lambda i: (0, 0))],
        out_specs=pl.BlockSpec((tg, d), lambda i: (i, 0)),
        out_shape=jax.ShapeDtypeStruct((n_rows, d), F32),
        scratch_shapes=[pltpu.VMEM((TOP_K, tg, d), F32), pltpu.SemaphoreType.DMA(())],
        compiler_params=_cparams("arbitrary"),
        name="moe_combine",
    )(dest_flat, y, x_all, mods, gates, final_g)


def _rope_tables(seq, tm):
    half = HEAD_DIM // 2
    inv = ROPE_THETA ** (-jnp.arange(0, half, 2, dtype=F32) / half)
    t = jnp.arange(seq)
    lane = np.arange(128)
    dd = lane % HEAD_DIM
    use_col = (dd // half) == 1
    j = dd % (half // 2)
    second = (dd % half) >= (half // 2)
    pos = jnp.where(jnp.asarray(use_col)[None, :], (t % GRID_W)[:, None], (t // GRID_W)[:, None]).astype(F32)
    ang = pos * inv[jnp.asarray(j)][None, :]
    cos = jnp.cos(ang)
    sin = jnp.sin(ang) * jnp.where(jnp.asarray(second), 1.0, -1.0)[None, :]
    cos = jnp.concatenate([cos, jnp.ones((tm, 128), F32)], axis=0)
    sin = jnp.concatenate([sin, jnp.zeros((tm, 128), F32)], axis=0)
    return cos, sin


def _moe_plan(idx, rank, counts, n_exp, n_rows):
    bm = BM
    nk = n_rows * TOP_K
    n_blocks = -(-nk // bm) + n_exp
    counts = counts.reshape(n_exp)
    padded = (counts + bm - 1) // bm * bm
    pend = jnp.cumsum(padded)
    pstart = pend - padded
    dest = (pstart[idx] + rank).reshape(nk).astype(jnp.int32)
    block_start = jnp.arange(n_blocks, dtype=jnp.int32) * bm
    block_e = jnp.minimum(jnp.sum(pend[None, :] <= block_start[:, None], axis=1), n_exp - 1).astype(jnp.int32)
    n_used = (pend[-1:] // bm).astype(jnp.int32)
    return dest, block_e, n_used, (pstart + counts).astype(jnp.int32), pend.astype(jnp.int32), n_blocks * bm


def kernel(x, c, ctx, c_ctx, ada_w, ada_b, norm1_g, norm2_g, w_in, conv_w, na_rpb, q_norm_g, k_norm_g,
           w_br_conv, w_br_na, w_br_gqa, w_out, router_w, router_b, w_gu, b_gu, w_down, b_down, final_g):
    n_batch, seq, d = x.shape
    n_ctx = ctx.shape[1]
    depth = ada_w.shape[0]
    n_exp = router_w.shape[-1]
    n_lat = n_batch * seq
    n_all = n_lat + n_batch * n_ctx
    assert n_batch + 1 <= MOD_ROWS and seq % TM == 0 and (n_batch * n_ctx) % TM == 0

    cond = jnp.zeros((MOD_ROWS, d), F32).at[:n_batch].set(c).at[n_batch].set(c_ctx)
    mods_all = ada_mods(cond, ada_w, ada_b)
    cos_t, sin_t = _rope_tables(seq, TM)
    hsum = jnp.asarray(np.kron(np.eye(128 // HEAD_DIM), np.ones((HEAD_DIM, HEAD_DIM))), BF16)

    f = w_gu.shape[-1] // 2
    wg, wu = split_gate_up(w_gu.reshape(depth * n_exp, d, 2 * f))
    bg = b_gu[..., 0::2].reshape(depth * n_exp, 1, f)
    bu = b_gu[..., 1::2].reshape(depth * n_exp, 1, f)
    wd = w_down.reshape(depth * n_exp, f, d).astype(BF16)
    bd = b_down.reshape(depth * n_exp, 1, d)

    x_all = jnp.concatenate([x.reshape(n_lat, d), ctx.reshape(n_batch * n_ctx, d)], axis=0)
    for l in range(depth):
        last = l == depth - 1
        mods = mods_all[l].reshape(MOD_ROWS, 1, 6 * d)
        qg = jnp.tile(q_norm_g[l], 128 // HEAD_DIM).reshape(1, 128)
        kg = jnp.tile(k_norm_g[l], 128 // HEAD_DIM).reshape(1, 128)
        conv, naq, nak, nav, gq, gk, gv, gates = in_projection(
            x_all, mods, norm1_g[l].reshape(1, d), w_in[l].astype(BF16), cos_t, sin_t, qg, kg, hsum,
            seq=seq, n_lat=n_lat, n_batch=n_batch)

        y_na = neighborhood_attention(naq, nak, nav, na_bias_table(na_rpb[l], seq),
                                      seq=seq, ctx=n_ctx, n_batch=n_batch)

        def kv_all(t):
            lat = t[:n_lat].reshape(n_batch, seq, GQA_KV_HEADS, HEAD_DIM)
            cx = t[n_lat:].reshape(n_batch, n_ctx, GQA_KV_HEADS, HEAD_DIM)
            return jnp.concatenate([lat, cx], axis=1)

        kt_all = kv_all(gk).transpose(0, 2, 3, 1)
        v_all = kv_all(gv).transpose(0, 2, 1, 3)
        v_all = jnp.concatenate([v_all, jnp.ones_like(v_all[..., :1]), jnp.zeros_like(v_all[..., 1:])], axis=-1)
        y_gqa = gqa_attention(gq, kt_all, v_all, seq=seq, n_batch=n_batch)

        n_rows = n_lat if last else n_all
        if not last:
            yc_na, yc_gqa = ctx_attention(naq, nak, nav, gq, gk, gv, ctx=n_ctx, n_lat=n_lat, n_batch=n_batch)
            y_na = jnp.concatenate([y_na, yc_na], axis=0)
            y_gqa = jnp.concatenate([y_gqa, yc_gqa], axis=0)
        x_all = merge_branches(x_all, mods, conv, y_na, y_gqa, gates, conv_w[l],
                               w_br_conv[l].astype(BF16), w_br_na[l].astype(BF16), w_br_gqa[l].astype(BF16),
                               w_out[l].astype(BF16), n_rows=n_rows, seq=seq, ctx=n_ctx, n_lat=n_lat,
                               n_batch=n_batch)

        h2, idx, gate_w, rank, counts = router(x_all, mods, norm2_g[l].reshape(1, d), router_w[l], router_b[l],
                                               n_rows=n_rows, seq=seq, n_batch=n_batch)
        dest, block_e, n_used, fill_lo, fill_hi, cap = _moe_plan(
            idx[:, :TOP_K], rank[:, :TOP_K], counts, n_exp, n_rows)
        xs = dispatch(h2, dest, fill_lo, fill_hi, cap)
        y = experts(xs, block_e + l * n_exp, n_used, wg, wu, bg, bu, wd, bd)
        x_all = combine(x_all, mods, y, dest, gate_w, final_g.reshape(1, d),
                        n_rows=n_rows, seq=seq, n_batch=n_batch, final_norm=last)
    return x_all.reshape(n_batch, seq, d)
```

```python
import functools

import jax
import jax.numpy as jnp
import numpy as np
from jax import lax
from jax.experimental import pallas as pl
from jax.experimental.pallas import tpu as pltpu

F32 = jnp.float32
BF16 = jnp.bfloat16

GRID_W = 64
EPS = 1e-6
HEAD_DIM = 64
CONV_WIDTH = 512
NA_HEADS = 8
NA_WIDTH = NA_HEADS * HEAD_DIM
NA_KH = 8
NA_KW = 16
NA_QROWS = 4
NA_BAND = NA_QROWS + NA_KH
GQA_Q_HEADS = 8
GQA_KV_HEADS = 2
GQA_GROUP = GQA_Q_HEADS // GQA_KV_HEADS
GQA_WIDTH = GQA_Q_HEADS * HEAD_DIM
GQA_KV_WIDTH = GQA_KV_HEADS * HEAD_DIM
ROPE_THETA = 10000.0
TOP_K = 4
SWIGLU_ALPHA = 1.702
SWIGLU_LIMIT = 7.0
GQA_Q_SCALE = HEAD_DIM ** -0.5 * float(np.log2(np.e))
MOD_ROWS = 16
NEG_BIG = -1e30
VMEM_LIMIT = 56 * 1024 * 1024

TM = 512
TQ = 256
GQA_KEY_CHUNK = 1024
BM = 512
TG = 256
ROW_UNROLL = 8


def _cparams(*sem):
    return pltpu.CompilerParams(dimension_semantics=sem, vmem_limit_bytes=VMEM_LIMIT)


def _rms(x):
    return x * lax.rsqrt(jnp.mean(x * x, axis=-1, keepdims=True) + EPS)


def _ada_kernel(c_ref, w_ref, b_ref, o_ref):
    c = c_ref[...]
    s = c * jax.nn.sigmoid(c)
    o_ref[0] = jnp.dot(s, w_ref[0], precision=lax.Precision.HIGHEST,
                       preferred_element_type=F32) + b_ref[0]


def ada_mods(cond, ada_w, ada_b):
    depth, d, n6 = ada_w.shape
    tn = d
    return pl.pallas_call(
        _ada_kernel,
        grid=(depth, n6 // tn),
        in_specs=[pl.BlockSpec((MOD_ROWS, d), lambda l, j: (0, 0)),
                  pl.BlockSpec((1, d, tn), lambda l, j: (l, 0, j)),
                  pl.BlockSpec((1, 1, tn), lambda l, j: (l, 0, j))],
        out_specs=pl.BlockSpec((1, MOD_ROWS, tn), lambda l, j: (l, 0, j)),
        out_shape=jax.ShapeDtypeStruct((depth, MOD_ROWS, n6), F32),
        compiler_params=_cparams("arbitrary", "arbitrary"),
        name="ada_mods",
    )(cond, ada_w, ada_b.reshape(depth, 1, n6))


def _head_inv_rms(t, hsum_ref):
    s = t * t
    s_hi = s.astype(BF16)
    s_lo = (s - s_hi.astype(F32)).astype(BF16)
    tot = (jnp.dot(s_hi, hsum_ref[...], preferred_element_type=F32)
           + jnp.dot(s_lo, hsum_ref[...], preferred_element_type=F32))
    return lax.rsqrt(tot * (1.0 / HEAD_DIM) + EPS)


def _rope(t, cos, sin):
    lane = lax.broadcasted_iota(jnp.int32, t.shape, 1)
    second = (lane % 32) >= 16
    partner = jnp.where(second, pltpu.roll(t, 16, 1), pltpu.roll(t, 128 - 16, 1))
    return t * cos + partner * sin


def _inproj_kernel(x_ref, sh_ref, sc_ref, g_ref, w_ref, cos_ref, sin_ref, qg_ref, kg_ref, hsum_ref,
                   conv_o, naq_o, nak_o, nav_o, gq_o, gk_o, gv_o, gate_o, h_scr, *, d):
    h = _rms(x_ref[...]) * g_ref[...]
    h = h * (1.0 + sc_ref[0]) + sh_ref[0]
    h_scr[...] = h.astype(BF16)

    def proj(a, n):
        return jnp.dot(h_scr[...], w_ref[:, a:a + n], preferred_element_type=F32)

    off = 0
    for j in range(3):
        conv_o[:, j * CONV_WIDTH:(j + 1) * CONV_WIDTH] = proj(off, CONV_WIDTH).astype(BF16)
        off += CONV_WIDTH
    naq_o[...] = (proj(off, NA_WIDTH) * (HEAD_DIM ** -0.5)).astype(BF16)
    off += NA_WIDTH
    nak_o[...] = proj(off, NA_WIDTH).astype(BF16)
    off += NA_WIDTH
    nav_o[...] = proj(off, NA_WIDTH).astype(BF16)
    off += NA_WIDTH
    cos = cos_ref[...]
    sin = sin_ref[...]
    for j in range(GQA_WIDTH // 128):
        t = proj(off + j * 128, 128)
        t = t * _head_inv_rms(t, hsum_ref) * qg_ref[...]
        gq_o[:, j * 128:(j + 1) * 128] = (_rope(t, cos, sin) * GQA_Q_SCALE).astype(BF16)
    off += GQA_WIDTH
    t = proj(off, GQA_KV_WIDTH)
    t = t * _head_inv_rms(t, hsum_ref) * kg_ref[...]
    gk_o[...] = _rope(t, cos, sin).astype(BF16)
    off += GQA_KV_WIDTH
    gv_o[...] = proj(off, GQA_KV_WIDTH).astype(BF16)
    off += GQA_KV_WIDTH
    for j in range(3):
        gate_o[:, j * d:(j + 1) * d] = jax.nn.sigmoid(proj(off + j * d, d)).astype(BF16)


def in_projection(x_all, mods, g1, w_in_bf, cos_t, sin_t, qg, kg, hsum, *, seq, n_lat, n_batch):
    nt, d = x_all.shape
    tm = TM
    n_lat_tiles = n_lat // tm
    seq_tiles = seq // tm
    in_cols = w_in_bf.shape[1]

    def mod_row(i):
        return jnp.minimum((i * tm) // seq, n_batch)

    def rope_blk(i):
        return jnp.where(i < n_lat_tiles, i % seq_tiles, seq_tiles)

    tok = lambda w: pl.BlockSpec((tm, w), lambda i: (i, 0))
    const = lambda shape: pl.BlockSpec(shape, lambda i: tuple(0 for _ in shape))
    outs = [CONV_WIDTH * 3, NA_WIDTH, NA_WIDTH, NA_WIDTH, GQA_WIDTH, GQA_KV_WIDTH, GQA_KV_WIDTH, 3 * d]
    return pl.pallas_call(
        functools.partial(_inproj_kernel, d=d),
        grid=(nt // tm,),
        in_specs=[tok(d),
                  pl.BlockSpec((1, 1, d), lambda i: (mod_row(i), 0, 0)),
                  pl.BlockSpec((1, 1, d), lambda i: (mod_row(i), 0, 1)),
                  const((1, d)),
                  pl.BlockSpec((d, in_cols), lambda i: (0, 0), pipeline_mode=pl.Buffered(1)),
                  pl.BlockSpec((tm, 128), lambda i: (rope_blk(i), 0)),
                  pl.BlockSpec((tm, 128), lambda i: (rope_blk(i), 0)),
                  const((1, 128)), const((1, 128)), const((128, 128))],
        out_specs=[tok(w) for w in outs],
        out_shape=[jax.ShapeDtypeStruct((nt, w), BF16) for w in outs],
        scratch_shapes=[pltpu.VMEM((tm, d), BF16)],
        compiler_params=_cparams("arbitrary"),
        name="in_projection",
    )(x_all, mods, mods, g1, w_in_bf, cos_t, sin_t, qg, kg, hsum)


def _softmax_pv(scores, values, exp=jnp.exp):
    m = functools.reduce(jnp.maximum, [jnp.max(s, axis=-1, keepdims=True) for s in scores])
    ps = [exp(s - m) for s in scores]
    l = functools.reduce(lambda a, b: a + b, [jnp.sum(p, axis=-1, keepdims=True) for p in ps])
    o = functools.reduce(lambda a, b: a + b,
                         [jnp.dot(p.astype(BF16), v, preferred_element_type=F32) for p, v in zip(ps, values)])
    return o / l


def _qk(q, k):
    return lax.dot_general(q, k, (((1,), (1,)), ((), ())), preferred_element_type=F32)


def _na_kernel(q_ref, k_ref, v_ref, kc_ref, vc_ref, bias_ref, o_ref, *, w, rows):
    j = pl.program_id(1)
    start = jnp.clip(j * NA_QROWS - NA_KH // 2, 0, rows - NA_BAND) * w
    start = pl.multiple_of(start, w)
    nkeys = NA_BAND * w
    for h in range(NA_HEADS):
        hs = slice(h * HEAD_DIM, (h + 1) * HEAD_DIM)
        q = q_ref[:, hs]
        kw = k_ref[pl.ds(start, nkeys), hs]
        vw = v_ref[pl.ds(start, nkeys), hs]
        s_win = _qk(q, kw) + bias_ref[0, h]
        s_ctx = _qk(q, kc_ref[:, hs])
        o_ref[:, hs] = _softmax_pv([s_win, s_ctx], [vw, vc_ref[:, hs]]).astype(BF16)


def neighborhood_attention(naq, nak, nav, bias, *, seq, ctx, n_batch):
    w = GRID_W
    rows = seq // w
    nj = rows // NA_QROWS
    nq = NA_QROWS * w
    n_lat = n_batch * seq
    ctx_blk0 = n_lat // ctx

    def case(j):
        return jnp.where(j == 0, 0, jnp.where(j == nj - 1, 2, 1))

    return pl.pallas_call(
        functools.partial(_na_kernel, w=w, rows=rows),
        grid=(n_batch, nj),
        in_specs=[pl.BlockSpec((nq, NA_WIDTH), lambda b, j: (b * nj + j, 0)),
                  pl.BlockSpec((seq, NA_WIDTH), lambda b, j: (b, 0)),
                  pl.BlockSpec((seq, NA_WIDTH), lambda b, j: (b, 0)),
                  pl.BlockSpec((ctx, NA_WIDTH), lambda b, j: (ctx_blk0 + b, 0)),
                  pl.BlockSpec((ctx, NA_WIDTH), lambda b, j: (ctx_blk0 + b, 0)),
                  pl.BlockSpec((1, NA_HEADS, nq, NA_BAND * w), lambda b, j: (case(j), 0, 0, 0))],
        out_specs=pl.BlockSpec((nq, NA_WIDTH), lambda b, j: (b * nj + j, 0)),
        out_shape=jax.ShapeDtypeStruct((n_lat, NA_WIDTH), BF16),
        compiler_params=_cparams("arbitrary", "arbitrary"),
        name="neighborhood_attention",
    )(naq, nak, nav, nak, nav, bias)


def na_bias_table(rpb, seq):
    w = GRID_W
    rows = seq // w
    nj = rows // NA_QROWS
    n_heads = rpb.shape[0]
    cols = np.arange(w)
    col_start = np.clip(cols - NA_KW // 2, 0, w - NA_KW)
    col_ok = (cols[None, :] >= col_start[:, None]) & (cols[None, :] < col_start[:, None] + NA_KW)
    pad = max(0, w - NA_KW)
    rp = jnp.pad(rpb.astype(F32), ((0, 0), (0, 0), (pad, pad)))
    tcol = jnp.stack([rp[:, :, pad + NA_KW - 1 - c: pad + NA_KW - 1 - c + w] for c in range(w)], axis=2)
    tcol = jnp.where(jnp.asarray(col_ok)[None, None], tcol, NEG_BIG)
    masked = jnp.full((n_heads, w, w), NEG_BIG, F32)
    tabs = []
    for j in (0, 1, nj - 1):
        start = int(np.clip(j * NA_QROWS - NA_KH // 2, 0, rows - NA_BAND))
        q_rows = []
        for a in range(NA_QROWS):
            r = j * NA_QROWS + a
            rs = int(np.clip(r - NA_KH // 2, 0, rows - NA_KH))
            blocks = []
            for i in range(NA_BAND):
                kr = start + i
                blocks.append(tcol[:, kr - r + NA_KH - 1] if rs <= kr < rs + NA_KH else masked)
            q_rows.append(jnp.concatenate(blocks, axis=-1))
        tabs.append(jnp.concatenate(q_rows, axis=1))
    return jnp.stack(tabs)


def _online_softmax_pv(q, kt_ref, v_ref, chunks):
    m = jnp.full((q.shape[0], 1), -jnp.inf, F32)
    acc = jnp.zeros((q.shape[0], v_ref.shape[-1]), F32)
    for a, n in chunks:
        s = jnp.dot(q, kt_ref[:, a:a + n], preferred_element_type=F32)
        m_new = jnp.maximum(m, jnp.max(s, axis=-1, keepdims=True))
        p = jnp.exp2(s - m_new).astype(BF16)
        acc = acc * jnp.exp2(m - m_new) + jnp.dot(p, v_ref[a:a + n, :], preferred_element_type=F32)
        m = m_new
    return acc[:, :HEAD_DIM] / acc[:, HEAD_DIM:HEAD_DIM + 1]


def _gqa_kernel(q_ref, kt_ref, v_ref, o_ref, *, tq, chunks):
    q = jnp.concatenate([q_ref[:, g * HEAD_DIM:(g + 1) * HEAD_DIM] for g in range(GQA_GROUP)], axis=0)
    o = _online_softmax_pv(q, kt_ref.at[0, 0], v_ref.at[0, 0], chunks)
    for g in range(GQA_GROUP):
        o_ref[:, g * HEAD_DIM:(g + 1) * HEAD_DIM] = o[g * tq:(g + 1) * tq].astype(BF16)


def _key_chunks(total, size):
    return tuple((a, min(size, total - a)) for a in range(0, total, size))


def gqa_attention(gq, kt_all, v_all, *, seq, n_batch):
    tq = TQ
    nq = seq // tq
    s_all = kt_all.shape[-1]
    vw = v_all.shape[-1]
    gw = GQA_GROUP * HEAD_DIM
    return pl.pallas_call(
        functools.partial(_gqa_kernel, tq=tq, chunks=_key_chunks(s_all, GQA_KEY_CHUNK)),
        grid=(n_batch, GQA_KV_HEADS, nq),
        in_specs=[pl.BlockSpec((tq, gw), lambda b, kv, i: (b * nq + i, kv)),
                  pl.BlockSpec((1, 1, HEAD_DIM, s_all), lambda b, kv, i: (b, kv, 0, 0)),
                  pl.BlockSpec((1, 1, s_all, vw), lambda b, kv, i: (b, kv, 0, 0))],
        out_specs=pl.BlockSpec((tq, gw), lambda b, kv, i: (b * nq + i, kv)),
        out_shape=jax.ShapeDtypeStruct((n_batch * seq, GQA_WIDTH), BF16),
        compiler_params=_cparams("arbitrary", "arbitrary", "arbitrary"),
        name="gqa_attention",
    )(gq, kt_all, v_all)


def _ctx_attn_kernel(naq_ref, nak_ref, nav_ref, gq_ref, gk_ref, gv_ref, ona_ref, ogqa_ref):
    for h in range(NA_HEADS):
        hs = slice(h * HEAD_DIM, (h + 1) * HEAD_DIM)
        ona_ref[:, hs] = _softmax_pv([_qk(naq_ref[:, hs], nak_ref[:, hs])], [nav_ref[:, hs]]).astype(BF16)
    for h in range(GQA_Q_HEADS):
        hs = slice(h * HEAD_DIM, (h + 1) * HEAD_DIM)
        kv = h // GQA_GROUP
        ks = slice(kv * HEAD_DIM, (kv + 1) * HEAD_DIM)
        ogqa_ref[:, hs] = _softmax_pv([_qk(gq_ref[:, hs], gk_ref[:, ks])], [gv_ref[:, ks]],
                                      exp=jnp.exp2).astype(BF16)


def ctx_attention(naq, nak, nav, gq, gk, gv, *, ctx, n_lat, n_batch):
    blk0 = n_lat // ctx
    wide = lambda: pl.BlockSpec((ctx, NA_WIDTH), lambda b: (blk0 + b, 0))
    narrow = lambda: pl.BlockSpec((ctx, GQA_KV_WIDTH), lambda b: (blk0 + b, 0))
    out = lambda: pl.BlockSpec((ctx, NA_WIDTH), lambda b: (b, 0))
    return pl.pallas_call(
        _ctx_attn_kernel,
        grid=(n_batch,),
        in_specs=[wide(), wide(), wide(), wide(), narrow(), narrow()],
        out_specs=[out(), out()],
        out_shape=[jax.ShapeDtypeStruct((n_batch * ctx, NA_WIDTH), BF16)] * 2,
        compiler_params=_cparams("arbitrary"),
        name="ctx_attention",
    )(naq, nak, nav, gq, gk, gv)


def _merge_kernel(x_ref, gt_ref, conv_ref, prev_ref, next_ref, yna_ref, ygqa_ref, gate_ref,
                  cw_ref, wc_ref, wn_ref, wg_ref, wo_ref, o_ref, *, tm, d, seq, ctx, n_lat):
    i = pl.program_id(0)
    cw = CONV_WIDTH

    def u_of(ref, rows):
        return ref[rows, cw:2 * cw].astype(F32) * ref[rows, 2 * cw:3 * cw].astype(F32)

    u = u_of(conv_ref, slice(None))
    hp = u_of(prev_ref, slice(15, 16))
    hn = u_of(next_ref, slice(0, 1))
    row = lax.broadcasted_iota(jnp.int32, (tm, 1), 0)
    r = i * tm + row
    pos = jnp.where(r < n_lat, r % seq, r % ctx)
    length = jnp.where(r < n_lat, seq, ctx)
    u_prev = jnp.where(row == 0, hp, pltpu.roll(u, 1, 0))
    u_prev = jnp.where(pos == 0, 0.0, u_prev)
    u_next = jnp.where(row == tm - 1, hn, pltpu.roll(u, tm - 1, 0))
    u_next = jnp.where(pos == length - 1, 0.0, u_next)
    y_conv = conv_ref[:, 0:cw].astype(F32) * (u_prev * cw_ref[0:1] + u * cw_ref[1:2] + u_next * cw_ref[2:3])

    m = gate_ref[:, 0:d].astype(F32) * jnp.dot(y_conv.astype(BF16), wc_ref[...], preferred_element_type=F32)
    m += gate_ref[:, d:2 * d].astype(F32) * jnp.dot(yna_ref[...], wn_ref[...], preferred_element_type=F32)
    m += gate_ref[:, 2 * d:3 * d].astype(F32) * jnp.dot(ygqa_ref[...], wg_ref[...], preferred_element_type=F32)
    out = jnp.dot(m.astype(BF16), wo_ref[...], preferred_element_type=F32)
    o_ref[...] = x_ref[...] + gt_ref[0] * out


def merge_branches(x_all, mods, conv, y_na, y_gqa, gates, conv_w, wc, wn, wg, wo, *, n_rows, seq, ctx,
                   n_lat, n_batch):
    d = x_all.shape[1]
    tm = TM
    halo = 16
    n_halo_blocks = conv.shape[0] // halo
    per = tm // halo

    def mod_row(i):
        return jnp.minimum((i * tm) // seq, n_batch)

    tok = lambda w: pl.BlockSpec((tm, w), lambda i: (i, 0))
    const = lambda shape: pl.BlockSpec(shape, lambda i: tuple(0 for _ in shape))
    return pl.pallas_call(
        functools.partial(_merge_kernel, tm=tm, d=d, seq=seq, ctx=ctx, n_lat=n_lat),
        grid=(n_rows // tm,),
        in_specs=[tok(d),
                  pl.BlockSpec((1, 1, d), lambda i: (mod_row(i), 0, 2)),
                  tok(3 * CONV_WIDTH),
                  pl.BlockSpec((halo, 3 * CONV_WIDTH), lambda i: (jnp.maximum(i * per - 1, 0), 0)),
                  pl.BlockSpec((halo, 3 * CONV_WIDTH),
                               lambda i: (jnp.minimum((i + 1) * per, n_halo_blocks - 1), 0)),
                  tok(NA_WIDTH), tok(GQA_WIDTH), tok(3 * d),
                  const((3, CONV_WIDTH)), const((CONV_WIDTH, d)), const((NA_WIDTH, d)),
                  const((GQA_WIDTH, d)), const((d, d))],
        out_specs=tok(d),
        out_shape=jax.ShapeDtypeStruct((n_rows, d), F32),
        compiler_params=_cparams("arbitrary"),
        name="merge_branches",
    )(x_all, mods, conv, conv, conv, y_na, y_gqa, gates, conv_w, wc, wn, wg, wo)


def _router_kernel(x_ref, sh_ref, sc_ref, g_ref, rw_ref, rb_ref, tri_ref,
                   h_o, idx_o, gate_o, rank_o, cnt_o, carry, *, n_exp):
    i = pl.program_id(0)

    @pl.when(i == 0)
    def _():
        carry[...] = jnp.zeros_like(carry)

    h = _rms(x_ref[...]) * g_ref[...]
    h = h * (1.0 + sc_ref[0]) + sh_ref[0]
    h_o[...] = h
    logits = jnp.dot(h, rw_ref[...], precision=lax.Precision.HIGHEST,
                     preferred_element_type=F32) + rb_ref[...]
    lane = lax.broadcasted_iota(jnp.int32, logits.shape, 1)
    work = logits
    vals, idxs, hots = [], [], []
    for _ in range(TOP_K):
        m = jnp.max(work, axis=-1, keepdims=True)
        ik = jnp.min(jnp.where(work == m, lane, n_exp), axis=-1, keepdims=True)
        hot = lane == ik
        vals.append(m)
        idxs.append(ik)
        hots.append(hot)
        work = jnp.where(hot, -jnp.inf, work)
    es = [jnp.exp(v - vals[0]) for v in vals]
    den = functools.reduce(lambda a, b: a + b, es)
    hot_sum = functools.reduce(lambda a, b: a + b, [jnp.where(hh, 1.0, 0.0) for hh in hots])
    prefix = jnp.dot(tri_ref[...], hot_sum.astype(BF16), preferred_element_type=F32) + carry[...]
    ranks = [jnp.sum(jnp.where(hh, prefix, 0.0), axis=-1, keepdims=True) for hh in hots]
    carry[...] = carry[...] + jnp.sum(hot_sum, axis=0, keepdims=True)
    cnt_o[...] = carry[...].astype(jnp.int32)

    out_lane = lax.broadcasted_iota(jnp.int32, idx_o.shape, 1)

    def spread(cols, fill):
        acc = jnp.full(idx_o.shape, fill, cols[0].dtype)
        for k, c in enumerate(cols):
            acc = jnp.where(out_lane == k, c, acc)
        return acc

    idx_o[...] = spread(idxs, 0)
    gate_o[...] = spread([e / den for e in es], 0.0)
    rank_o[...] = spread([rk.astype(jnp.int32) for rk in ranks], 0)


def router(x_all, mods, g2, router_w, router_b, *, n_rows, seq, n_batch):
    d = x_all.shape[1]
    n_exp = router_w.shape[1]
    tm = TM
    tri = jnp.asarray(np.tril(np.ones((tm, tm), np.float32), -1), BF16)

    def mod_row(i):
        return jnp.minimum((i * tm) // seq, n_batch)

    tok = lambda w: pl.BlockSpec((tm, w), lambda i: (i, 0))
    const = lambda shape: pl.BlockSpec(shape, lambda i: tuple(0 for _ in shape))
    return pl.pallas_call(
        functools.partial(_router_kernel, n_exp=n_exp),
        grid=(n_rows // tm,),
        in_specs=[tok(d),
                  pl.BlockSpec((1, 1, d), lambda i: (mod_row(i), 0, 3)),
                  pl.BlockSpec((1, 1, d), lambda i: (mod_row(i), 0, 4)),
                  const((1, d)), const((d, n_exp)), const((1, n_exp)), const((tm, tm))],
        out_specs=[tok(d), tok(128), tok(128), tok(128), const((1, n_exp))],
        out_shape=[jax.ShapeDtypeStruct((n_rows, d), F32),
                   jax.ShapeDtypeStruct((n_rows, 128), jnp.int32),
                   jax.ShapeDtypeStruct((n_rows, 128), F32),
                   jax.ShapeDtypeStruct((n_rows, 128), jnp.int32),
                   jax.ShapeDtypeStruct((1, n_exp), jnp.int32)],
        scratch_shapes=[pltpu.VMEM((1, n_exp), F32)],
        compiler_params=_cparams("arbitrary"),
        name="router",
    )(x_all, mods, mods, g2, router_w, router_b.reshape(1, n_exp), tri)


def _dispatch_kernel(fill_lo_ref, fill_hi_ref, dest_ref, h_ref, xs_ref, sem, *, tg, n_exp):
    i = pl.program_id(0)

    def row_copy(t, slot):
        return pltpu.make_async_copy(h_ref.at[pl.ds(t, 1)], xs_ref.at[pl.ds(slot, 1)], sem)

    def issue(g, c):
        for j in range(ROW_UNROLL):
            t = g * ROW_UNROLL + j
            for k in range(TOP_K):
                row_copy(t, dest_ref[t * TOP_K + k]).start(priority=k % 2)
        return c

    lax.fori_loop(0, tg // ROW_UNROLL, issue, 0)

    def drain(g, c):
        for _ in range(ROW_UNROLL * TOP_K):
            row_copy(0, 0).wait()
        return c

    lax.fori_loop(0, tg // ROW_UNROLL, drain, 0)

    @pl.when(i == pl.num_programs(0) - 1)
    def _():
        for e in range(n_exp):
            lo = fill_lo_ref[e]
            hi = fill_hi_ref[e]

            def fill(p, c):
                row_copy(0, p).start()
                return c

            lax.fori_loop(lo, hi, fill, 0)

            def fill_wait(p, c):
                row_copy(0, 0).wait()
                return c

            lax.fori_loop(lo, hi, fill_wait, 0)

        def tail_copy(blk):
            return pltpu.make_async_copy(h_ref, xs_ref.at[pl.ds(pl.multiple_of(blk * tg, tg), tg)], sem)

        first_free = fill_hi_ref[n_exp - 1] // tg

        def tail(blk, c):
            tail_copy(blk).start()
            tail_copy(blk).wait()
            return c

        lax.fori_loop(first_free, xs_ref.shape[0] // tg, tail, 0)


def dispatch(h2, dest_flat, fill_lo, fill_hi, cap):
    n, d = h2.shape
    tg = TG
    assert BM % tg == 0 and cap % tg == 0
    n_exp = fill_lo.shape[0]
    return pl.pallas_call(
        functools.partial(_dispatch_kernel, tg=tg, n_exp=n_exp),
        grid_spec=pltpu.PrefetchScalarGridSpec(
            num_scalar_prefetch=2,
            grid=(n // tg,),
            in_specs=[pl.BlockSpec((tg * TOP_K,), lambda i, lo, hi: (i,), memory_space=pltpu.SMEM),
                      pl.BlockSpec((tg, d), lambda i, lo, hi: (i, 0))],
            out_specs=pl.BlockSpec(memory_space=pl.ANY),
            scratch_shapes=[pltpu.SemaphoreType.DMA(())]),
        out_shape=jax.ShapeDtypeStruct((cap, d), F32),
        compiler_params=_cparams("arbitrary"),
        name="moe_dispatch",
    )(fill_lo, fill_hi, dest_flat, h2)


def _deinterleave_kernel(w_ref, sel_ref, g_ref, u_ref):
    for c in range(w_ref.shape[-1] // 256):
        t = jnp.dot(w_ref[0, :, c * 256:(c + 1) * 256].astype(BF16), sel_ref[...], preferred_element_type=F32)
        g_ref[0, :, c * 128:(c + 1) * 128] = t[:, :128].astype(BF16)
        u_ref[0, :, c * 128:(c + 1) * 128] = t[:, 128:].astype(BF16)


def split_gate_up(w_gu):
    n, d, f2 = w_gu.shape
    sel = np.zeros((256, 256), np.float32)
    sel[2 * np.arange(128), np.arange(128)] = 1.0
    sel[2 * np.arange(128) + 1, 128 + np.arange(128)] = 1.0
    out = pl.BlockSpec((1, d, f2 // 2), lambda e: (e, 0, 0))
    return pl.pallas_call(
        _deinterleave_kernel,
        grid=(n,),
        in_specs=[pl.BlockSpec((1, d, f2), lambda e: (e, 0, 0)),
                  pl.BlockSpec((256, 256), lambda e: (0, 0))],
        out_specs=[out, out],
        out_shape=[jax.ShapeDtypeStruct((n, d, f2 // 2), BF16)] * 2,
        compiler_params=_cparams("arbitrary"),
        name="split_gate_up",
    )(w_gu, jnp.asarray(sel, BF16))


def _expert_kernel(be_ref, nu_ref, x_ref, wg_ref, wu_ref, bg_ref, bu_ref, wd_ref, bd_ref, y_ref):
    m = pl.program_id(0)

    @pl.when(m < nu_ref[0])
    def _():
        x = x_ref[...].astype(BF16)
        hg = jnp.dot(x, wg_ref[0], preferred_element_type=F32) + bg_ref[0]
        hu = jnp.dot(x, wu_ref[0], preferred_element_type=F32) + bu_ref[0]
        g = jnp.minimum(hg, SWIGLU_LIMIT)
        u = jnp.clip(hu, -SWIGLU_LIMIT, SWIGLU_LIMIT)
        a = (u + 1.0) * (g * jax.nn.sigmoid(SWIGLU_ALPHA * g))
        y_ref[...] = jnp.dot(a.astype(BF16), wd_ref[0], preferred_element_type=F32) + bd_ref[0]

    @pl.when(m >= nu_ref[0])
    def _():
        y_ref[...] = jnp.zeros_like(y_ref)


def experts(xs, block_e, n_used, wg, wu, bg, bu, wd, bd):
    cap, d = xs.shape
    bm = BM
    n_exp, _, f = wg.shape
    n_blocks = cap // bm

    def blk(m, be, nu):
        return jnp.minimum(m, nu[0] - 1)

    def exp_of(m, be, nu):
        return be[blk(m, be, nu)]

    return pl.pallas_call(
        _expert_kernel,
        grid_spec=pltpu.PrefetchScalarGridSpec(
            num_scalar_prefetch=2,
            grid=(n_blocks,),
            in_specs=[pl.BlockSpec((bm, d), lambda m, be, nu: (blk(m, be, nu), 0)),
                      pl.BlockSpec((1, d, f), lambda m, be, nu: (exp_of(m, be, nu), 0, 0)),
                      pl.BlockSpec((1, d, f), lambda m, be, nu: (exp_of(m, be, nu), 0, 0)),
                      pl.BlockSpec((1, 1, f), lambda m, be, nu: (exp_of(m, be, nu), 0, 0)),
                      pl.BlockSpec((1, 1, f), lambda m, be, nu: (exp_of(m, be, nu), 0, 0)),
                      pl.BlockSpec((1, f, d), lambda m, be, nu: (exp_of(m, be, nu), 0, 0)),
                      pl.BlockSpec((1, 1, d), lambda m, be, nu: (exp_of(m, be, nu), 0, 0))],
            out_specs=pl.BlockSpec((bm, d), lambda m, be, nu: (m, 0))),
        out_shape=jax.ShapeDtypeStruct((cap, d), F32),
        compiler_params=_cparams("arbitrary"),
        name="moe_experts",
    )(block_e, n_used, xs, wg, wu, bg, bu, wd, bd)


def _combine_kernel(dest_ref, y_ref, x_ref, gt_ref, gate_ref, fg_ref, o_ref, buf, sem, *, tg, final_norm):
    def row_copy(slot, k, t):
        return pltpu.make_async_copy(y_ref.at[pl.ds(slot, 1)], buf.at[k, pl.ds(t, 1)], sem)

    def issue(g, c):
        for j in range(ROW_UNROLL):
            t = g * ROW_UNROLL + j
            for k in range(TOP_K):
                row_copy(dest_ref[t * TOP_K + k], k, t).start(priority=k % 2)
        return c

    lax.fori_loop(0, tg // ROW_UNROLL, issue, 0)

    def drain(g, c):
        for _ in range(ROW_UNROLL * TOP_K):
            row_copy(0, 0, 0).wait()
        return c

    lax.fori_loop(0, tg // ROW_UNROLL, drain, 0)

    acc = gate_ref[:, 0:1] * buf[0]
    for k in range(1, TOP_K):
        acc += gate_ref[:, k:k + 1] * buf[k]
    xn = x_ref[...] + gt_ref[0] * acc
    if final_norm:
        xn = _rms(xn) * fg_ref[...]
    o_ref[...] = xn


def combine(x_all, mods, y, dest_flat, gates, final_g, *, n_rows, seq, n_batch, final_norm):
    d = x_all.shape[1]
    tg = TG

    def mod_row(i):
        return jnp.minimum((i * tg) // seq, n_batch)

    return pl.pallas_call(
        functools.partial(_combine_kernel, tg=tg, final_norm=final_norm),
        grid=(n_rows // tg,),
        in_specs=[pl.BlockSpec((tg * TOP_K,), lambda i: (i,), memory_space=pltpu.SMEM),
                  pl.BlockSpec(memory_space=pl.ANY),
                  pl.BlockSpec((tg, d), lambda i: (i, 0)),
                  pl.BlockSpec((1, 1, d), lambda i: (mod_row(i), 0, 5)),
                  pl.BlockSpec((tg, 128), lambda i: (i, 0)),
                  pl.BlockSpec((1, d), lambda i: (0, 0))],
        out_specs=pl.BlockSpec((tg, d), lambda i: (i, 0)),
        out_shape=jax.ShapeDtypeStruct((n_rows, d), F32),
        scratch_shapes=[pltpu.VMEM((TOP_K, tg, d), F32), pltpu.SemaphoreType.DMA(())],
        compiler_params=_cparams("arbitrary"),
        name="moe_combine",
    )(dest_flat, y, x_all, mods, gates, final_g)


def _rope_tables(seq, tm):
    half = HEAD_DIM // 2
    inv = ROPE_THETA ** (-jnp.arange(0, half, 2, dtype=F32) / half)
    t = jnp.arange(seq)
    lane = np.arange(128)
    dd = lane % HEAD_DIM
    use_col = (dd // half) == 1
    j = dd % (half // 2)
    second = (dd % half) >= (half // 2)
    pos = jnp.where(jnp.asarray(use_col)[None, :], (t % GRID_W)[:, None], (t // GRID_W)[:, None]).astype(F32)
    ang = pos * inv[jnp.asarray(j)][None, :]
    cos = jnp.cos(ang)
    sin = jnp.sin(ang) * jnp.where(jnp.asarray(second), 1.0, -1.0)[None, :]
    cos = jnp.concatenate([cos, jnp.ones((tm, 128), F32)], axis=0)
    sin = jnp.concatenate([sin, jnp.zeros((tm, 128), F32)], axis=0)
    return cos, sin


def _moe_plan(idx, rank, counts, n_exp, n_rows):
    bm = BM
    nk = n_rows * TOP_K
    n_blocks = -(-nk // bm) + n_exp
    counts = counts.reshape(n_exp)
    padded = (counts + bm - 1) // bm * bm
    pend = jnp.cumsum(padded)
    pstart = pend - padded
    dest = (pstart[idx] + rank).reshape(nk).astype(jnp.int32)
    block_start = jnp.arange(n_blocks, dtype=jnp.int32) * bm
    block_e = jnp.minimum(jnp.sum(pend[None, :] <= block_start[:, None], axis=1), n_exp - 1).astype(jnp.int32)
    n_used = (pend[-1:] // bm).astype(jnp.int32)
    return dest, block_e, n_used, (pstart + counts).astype(jnp.int32), pend.astype(jnp.int32), n_blocks * bm


def kernel(x, c, ctx, c_ctx, ada_w, ada_b, norm1_g, norm2_g, w_in, conv_w, na_rpb, q_norm_g, k_norm_g,
           w_br_conv, w_br_na, w_br_gqa, w_out, router_w, router_b, w_gu, b_gu, w_down, b_down, final_g):
    n_batch, seq, d = x.shape
    n_ctx = ctx.shape[1]
    depth = ada_w.shape[0]
    n_exp = router_w.shape[-1]
    n_lat = n_batch * seq
    n_all = n_lat + n_batch * n_ctx
    assert n_batch + 1 <= MOD_ROWS and seq % TM == 0 and (n_batch * n_ctx) % TM == 0

    cond = jnp.zeros((MOD_ROWS, d), F32).at[:n_batch].set(c).at[n_batch].set(c_ctx)
    mods_all = ada_mods(cond, ada_w, ada_b)
    cos_t, sin_t = _rope_tables(seq, TM)
    hsum = jnp.asarray(np.kron(np.eye(128 // HEAD_DIM), np.ones((HEAD_DIM, HEAD_DIM))), BF16)

    f = w_gu.shape[-1] // 2
    wg, wu = split_gate_up(w_gu.reshape(depth * n_exp, d, 2 * f))
    bg = b_gu[..., 0::2].reshape(depth * n_exp, 1, f)
    bu = b_gu[..., 1::2].reshape(depth * n_exp, 1, f)
    wd = w_down.reshape(depth * n_exp, f, d).astype(BF16)
    bd = b_down.reshape(depth * n_exp, 1, d)

    x_all = jnp.concatenate([x.reshape(n_lat, d), ctx.reshape(n_batch * n_ctx, d)], axis=0)
    for l in range(depth):
        last = l == depth - 1
        mods = mods_all[l].reshape(MOD_ROWS, 1, 6 * d)
        qg = jnp.tile(q_norm_g[l], 128 // HEAD_DIM).reshape(1, 128)
        kg = jnp.tile(k_norm_g[l], 128 // HEAD_DIM).reshape(1, 128)
        conv, naq, nak, nav, gq, gk, gv, gates = in_projection(
            x_all, mods, norm1_g[l].reshape(1, d), w_in[l].astype(BF16), cos_t, sin_t, qg, kg, hsum,
            seq=seq, n_lat=n_lat, n_batch=n_batch)

        y_na = neighborhood_attention(naq, nak, nav, na_bias_table(na_rpb[l], seq),
                                      seq=seq, ctx=n_ctx, n_batch=n_batch)

        def kv_all(t):
            lat = t[:n_lat].reshape(n_batch, seq, GQA_KV_HEADS, HEAD_DIM)
            cx = t[n_lat:].reshape(n_batch, n_ctx, GQA_KV_HEADS, HEAD_DIM)
            return jnp.concatenate([lat, cx], axis=1)

        kt_all = kv_all(gk).transpose(0, 2, 3, 1)
        v_all = kv_all(gv).transpose(0, 2, 1, 3)
        v_all = jnp.concatenate([v_all, jnp.ones_like(v_all[..., :1]), jnp.zeros_like(v_all[..., 1:])], axis=-1)
        y_gqa = gqa_attention(gq, kt_all, v_all, seq=seq, n_batch=n_batch)

        n_rows = n_lat if last else n_all
        if not last:
            yc_na, yc_gqa = ctx_attention(naq, nak, nav, gq, gk, gv, ctx=n_ctx, n_lat=n_lat, n_batch=n_batch)
            y_na = jnp.concatenate([y_na, yc_na], axis=0)
            y_gqa = jnp.concatenate([y_gqa, yc_gqa], axis=0)
        x_all = merge_branches(x_all, mods, conv, y_na, y_gqa, gates, conv_w[l],
                               w_br_conv[l].astype(BF16), w_br_na[l].astype(BF16), w_br_gqa[l].astype(BF16),
                               w_out[l].astype(BF16), n_rows=n_rows, seq=seq, ctx=n_ctx, n_lat=n_lat,
                               n_batch=n_batch)

        h2, idx, gate_w, rank, counts = router(x_all, mods, norm2_g[l].reshape(1, d), router_w[l], router_b[l],
                                               n_rows=n_rows, seq=seq, n_batch=n_batch)
        dest, block_e, n_used, fill_lo, fill_hi, cap = _moe_plan(
            idx[:, :TOP_K], rank[:, :TOP_K], counts, n_exp, n_rows)
        xs = dispatch(h2, dest, fill_lo, fill_hi, cap)
        y = experts(xs, block_e + l * n_exp, n_used, wg, wu, bg, bu, wd, bd)
        x_all = combine(x_all, mods, y, dest, gate_w, final_g.reshape(1, d),
                        n_rows=n_rows, seq=seq, n_batch=n_batch, final_norm=last)
    return x_all.reshape(n_batch, seq, d)
```

```python
import functools

import jax
import jax.numpy as jnp
import numpy as np
from jax import lax
from jax.experimental import pallas as pl
from jax.experimental.pallas import tpu as pltpu

F32 = jnp.float32
BF16 = jnp.bfloat16

GRID_W = 64
EPS = 1e-6
HEAD_DIM = 64
CONV_WIDTH = 512
NA_HEADS = 8
NA_WIDTH = NA_HEADS * HEAD_DIM
NA_KH = 8
NA_KW = 16
NA_QROWS = 4
NA_BAND = NA_QROWS + NA_KH
GQA_Q_HEADS = 8
GQA_KV_HEADS = 2
GQA_GROUP = GQA_Q_HEADS // GQA_KV_HEADS
GQA_WIDTH = GQA_Q_HEADS * HEAD_DIM
GQA_KV_WIDTH = GQA_KV_HEADS * HEAD_DIM
ROPE_THETA = 10000.0
TOP_K = 4
SWIGLU_ALPHA = 1.702
SWIGLU_LIMIT = 7.0
LOG2E = float(np.log2(np.e))
ATTN_Q_SCALE = HEAD_DIM ** -0.5 * LOG2E
GQA_Q_SCALE = ATTN_Q_SCALE
FP8 = jnp.float8_e4m3fn
GQA_K_FP8_SCALE = 0.125
MOD_ROWS = 16
NEG_BIG = -1e30
VMEM_LIMIT = 56 * 1024 * 1024

TM = 512
TQ = 256
GQA_KEY_CHUNK = 2048
BM = 512
TG = 256
ROW_UNROLL = 8


def _cparams(*sem):
    return pltpu.CompilerParams(dimension_semantics=sem, vmem_limit_bytes=VMEM_LIMIT)


def _rms(x):
    return x * lax.rsqrt(jnp.mean(x * x, axis=-1, keepdims=True) + EPS)


def _dot_bf16x3(a, b):
    a_hi = a.astype(BF16)
    b_hi = b.astype(BF16)
    a_lo = (a - a_hi.astype(F32)).astype(BF16)
    b_lo = (b - b_hi.astype(F32)).astype(BF16)
    dot = functools.partial(jnp.dot, preferred_element_type=F32)
    return dot(a_hi, b_hi) + (dot(a_lo, b_hi) + dot(a_hi, b_lo))


def _two_part_specs(tm, width, n_first):
    return [pl.BlockSpec((tm, width), lambda i: (jnp.minimum(i, n_first - 1), 0)),
            pl.BlockSpec((tm, width), lambda i: (jnp.maximum(i - n_first, 0), 0))]


def _two_part_read(a_ref, b_ref, n_first):
    return jnp.where(pl.program_id(0) < n_first, a_ref[...], b_ref[...])


def _two_parts(parts, tm):
    if len(parts) == 1:
        return parts[0], parts[0], parts[0].shape[0] // tm, parts[0].shape[0]
    return parts[0], parts[1], parts[0].shape[0] // tm, parts[0].shape[0] + parts[1].shape[0]


def _ada_kernel(c_ref, w_ref, b_ref, o_ref):
    c = c_ref[...]
    s = c * jax.nn.sigmoid(c)
    o_ref[0] = jnp.dot(s, w_ref[0], precision=lax.Precision.HIGHEST,
                       preferred_element_type=F32) + b_ref[0]


def ada_mods(cond, ada_w, ada_b):
    depth, d, n6 = ada_w.shape
    tn = d
    return pl.pallas_call(
        _ada_kernel,
        grid=(depth, n6 // tn),
        in_specs=[pl.BlockSpec((MOD_ROWS, d), lambda l, j: (0, 0)),
                  pl.BlockSpec((1, d, tn), lambda l, j: (l, 0, j)),
                  pl.BlockSpec((1, 1, tn), lambda l, j: (l, 0, j))],
        out_specs=pl.BlockSpec((1, MOD_ROWS, tn), lambda l, j: (l, 0, j)),
        out_shape=jax.ShapeDtypeStruct((depth, MOD_ROWS, n6), F32),
        compiler_params=_cparams("arbitrary", "arbitrary"),
        name="ada_mods",
    )(cond, ada_w, ada_b.reshape(depth, 1, n6))


def _head_inv_rms(t, hsum_ref):
    tot = jnp.dot((t * t).astype(BF16), hsum_ref[...], preferred_element_type=F32)
    return lax.rsqrt(tot * (1.0 / HEAD_DIM) + EPS)


def _rope(t, cos, sin):
    lane = lax.broadcasted_iota(jnp.int32, t.shape, 1)
    second = (lane % 32) >= 16
    partner = jnp.where(second, pltpu.roll(t, 16, 1), pltpu.roll(t, 128 - 16, 1))
    return t * cos + partner * sin


def _inproj_kernel(xa_ref, xb_ref, sh_ref, sc_ref, g_ref, w_ref, cos_ref, sin_ref, qg_ref, kg_ref, hsum_ref,
                   conv_o, naq_o, nak_o, nav_o, gq_o, gk_o, gv_o, gate_o, h_scr, *, d, n_first):
    h = _rms(_two_part_read(xa_ref, xb_ref, n_first)) * g_ref[...]
    h = h * (1.0 + sc_ref[0]) + sh_ref[0]
    h_scr[...] = h.astype(BF16)

    def proj(a, n):
        return jnp.dot(h_scr[...], w_ref[:, a:a + n], preferred_element_type=F32)

    off = 0
    for j in range(3):
        conv_o[:, j * CONV_WIDTH:(j + 1) * CONV_WIDTH] = proj(off, CONV_WIDTH).astype(BF16)
        off += CONV_WIDTH
    naq_o[...] = (proj(off, NA_WIDTH) * ATTN_Q_SCALE).astype(BF16)
    off += NA_WIDTH
    nak_o[...] = proj(off, NA_WIDTH).astype(BF16)
    off += NA_WIDTH
    nav_o[...] = proj(off, NA_WIDTH).astype(BF16)
    off += NA_WIDTH
    cos = cos_ref[...]
    sin = sin_ref[...]
    for j in range(GQA_WIDTH // 256):
        t = proj(off + j * 256, 256)
        t = t * _head_inv_rms(t, hsum_ref) * qg_ref[...]
        for half in range(2):
            lanes = slice(half * 128, (half + 1) * 128)
            gq_o[:, j * 256 + half * 128:j * 256 + (half + 1) * 128] = (
                _rope(t[:, lanes], cos, sin) * GQA_Q_SCALE).astype(BF16)
    off += GQA_WIDTH
    t = proj(off, 2 * GQA_KV_WIDTH)
    inv = _head_inv_rms(t, hsum_ref)
    gk_o[...] = _rope(t[:, :GQA_KV_WIDTH] * inv[:, :GQA_KV_WIDTH] * kg_ref[...], cos, sin).astype(BF16)
    gv_o[...] = t[:, GQA_KV_WIDTH:].astype(BF16)
    off += 2 * GQA_KV_WIDTH
    for j in range(3):
        gate_o[:, j * d:(j + 1) * d] = jax.nn.sigmoid(proj(off + j * d, d)).astype(BF16)


def in_projection(x_parts, mods, g1, w_in_bf, cos_t, sin_t, qg, kg, hsum, *, seq, n_lat, n_batch):
    tm = TM
    x_a, x_b, n_first, nt = _two_parts(x_parts, tm)
    d = x_a.shape[1]
    n_lat_tiles = n_lat // tm
    seq_tiles = seq // tm
    in_cols = w_in_bf.shape[1]

    def mod_row(i):
        return jnp.minimum((i * tm) // seq, n_batch)

    def rope_blk(i):
        return jnp.where(i < n_lat_tiles, i % seq_tiles, seq_tiles)

    tok = lambda w: pl.BlockSpec((tm, w), lambda i: (i, 0))
    const = lambda shape: pl.BlockSpec(shape, lambda i: tuple(0 for _ in shape))
    outs = [CONV_WIDTH * 3, NA_WIDTH, NA_WIDTH, NA_WIDTH, GQA_WIDTH, GQA_KV_WIDTH, GQA_KV_WIDTH, 3 * d]
    return pl.pallas_call(
        functools.partial(_inproj_kernel, d=d, n_first=n_first),
        grid=(nt // tm,),
        in_specs=_two_part_specs(tm, d, n_first) + [
                  pl.BlockSpec((1, 1, d), lambda i: (mod_row(i), 0, 0)),
                  pl.BlockSpec((1, 1, d), lambda i: (mod_row(i), 0, 1)),
                  const((1, d)),
                  pl.BlockSpec((d, in_cols), lambda i: (0, 0), pipeline_mode=pl.Buffered(1)),
                  pl.BlockSpec((tm, 128), lambda i: (rope_blk(i), 0)),
                  pl.BlockSpec((tm, 128), lambda i: (rope_blk(i), 0)),
                  const((1, 256)), const((1, 128)), const((256, 256))],
        out_specs=[tok(w) for w in outs],
        out_shape=[jax.ShapeDtypeStruct((nt, w), BF16) for w in outs],
        scratch_shapes=[pltpu.VMEM((tm, d), BF16)],
        compiler_params=_cparams("arbitrary"),
        name="in_projection",
    )(x_a, x_b, mods, mods, g1, w_in_bf, cos_t, sin_t, qg, kg, hsum)


def _softmax_pv(scores, values, exp=jnp.exp):
    m = functools.reduce(jnp.maximum, [jnp.max(s, axis=-1, keepdims=True) for s in scores])
    ps = [exp(s - m) for s in scores]
    l = functools.reduce(lambda a, b: a + b, [jnp.sum(p, axis=-1, keepdims=True) for p in ps])
    o = functools.reduce(lambda a, b: a + b,
                         [jnp.dot(p.astype(BF16), v, preferred_element_type=F32) for p, v in zip(ps, values)])
    return o / l


def _qk(q, k):
    return lax.dot_general(q, k, (((1,), (1,)), ((), ())), preferred_element_type=F32)


def _na_kernel(q_ref, k_ref, v_ref, kc_ref, vc_ref, bias_ref, o_ref, *, w, rows):
    j = pl.program_id(1)
    start = jnp.clip(j * NA_QROWS - NA_KH // 2, 0, rows - NA_BAND) * w
    start = pl.multiple_of(start, w)
    nkeys = NA_BAND * w
    first_head = lax.broadcasted_iota(jnp.int32, (1, 2 * HEAD_DIM), 1) < HEAD_DIM
    for pair in range(NA_HEADS // 2):
        ls = slice(pair * 2 * HEAD_DIM, (pair + 1) * 2 * HEAD_DIM)
        q2 = q_ref[:, ls]
        kw = k_ref[pl.ds(start, nkeys), ls]
        vw = v_ref[pl.ds(start, nkeys), ls]
        kc = kc_ref[:, ls]
        vc = vc_ref[:, ls]
        outs = []
        for e in range(2):
            q = jnp.where(first_head == (e == 0), q2, jnp.zeros_like(q2))
            s_win = _qk(q, kw) + bias_ref[0, 2 * pair + e]
            s_ctx = _qk(q, kc)
            outs.append(_softmax_pv([s_win, s_ctx], [vw, vc], exp=jnp.exp2))
        o_ref[:, ls] = jnp.where(first_head, outs[0], outs[1]).astype(BF16)


def neighborhood_attention(naq, nak, nav, bias, *, seq, ctx, n_batch):
    w = GRID_W
    rows = seq // w
    nj = rows // NA_QROWS
    nq = NA_QROWS * w
    n_lat = n_batch * seq
    ctx_blk0 = n_lat // ctx

    def case(j):
        return jnp.where(j == 0, 0, jnp.where(j == nj - 1, 2, 1))

    return pl.pallas_call(
        functools.partial(_na_kernel, w=w, rows=rows),
        grid=(n_batch, nj),
        in_specs=[pl.BlockSpec((nq, NA_WIDTH), lambda b, j: (b * nj + j, 0)),
                  pl.BlockSpec((seq, NA_WIDTH), lambda b, j: (b, 0)),
                  pl.BlockSpec((seq, NA_WIDTH), lambda b, j: (b, 0)),
                  pl.BlockSpec((ctx, NA_WIDTH), lambda b, j: (ctx_blk0 + b, 0)),
                  pl.BlockSpec((ctx, NA_WIDTH), lambda b, j: (ctx_blk0 + b, 0)),
                  pl.BlockSpec((1, NA_HEADS, nq, NA_BAND * w), lambda b, j: (case(j), 0, 0, 0))],
        out_specs=pl.BlockSpec((nq, NA_WIDTH), lambda b, j: (b * nj + j, 0)),
        out_shape=jax.ShapeDtypeStruct((n_lat, NA_WIDTH), BF16),
        compiler_params=_cparams("arbitrary", "arbitrary"),
        name="neighborhood_attention",
    )(naq, nak, nav, nak, nav, bias)


def na_bias_table(rpb, seq):
    w = GRID_W
    rows = seq // w
    nj = rows // NA_QROWS
    n_heads = rpb.shape[0]
    cols = np.arange(w)
    col_start = np.clip(cols - NA_KW // 2, 0, w - NA_KW)
    col_ok = (cols[None, :] >= col_start[:, None]) & (cols[None, :] < col_start[:, None] + NA_KW)
    pad = max(0, w - NA_KW)
    rp = jnp.pad(rpb.astype(F32), ((0, 0), (0, 0), (pad, pad)))
    tcol = jnp.stack([rp[:, :, pad + NA_KW - 1 - c: pad + NA_KW - 1 - c + w] for c in range(w)], axis=2)
    tcol = jnp.where(jnp.asarray(col_ok)[None, None], tcol, NEG_BIG)
    masked = jnp.full((n_heads, w, w), NEG_BIG, F32)
    tabs = []
    for j in (0, 1, nj - 1):
        start = int(np.clip(j * NA_QROWS - NA_KH // 2, 0, rows - NA_BAND))
        q_rows = []
        for a in range(NA_QROWS):
            r = j * NA_QROWS + a
            rs = int(np.clip(r - NA_KH // 2, 0, rows - NA_KH))
            blocks = []
            for i in range(NA_BAND):
                kr = start + i
                blocks.append(tcol[:, kr - r + NA_KH - 1] if rs <= kr < rs + NA_KH else masked)
            q_rows.append(jnp.concatenate(blocks, axis=-1))
        tabs.append(jnp.concatenate(q_rows, axis=1))
    return jnp.stack(tabs)


def _online_softmax_pv(q, kt_ref, v_ref, chunks):
    m = jnp.full((q.shape[0], 1), -jnp.inf, F32)
    acc = jnp.zeros((q.shape[0], v_ref.shape[-1]), F32)
    for a, n in chunks:
        s = jnp.dot(q, kt_ref[:, a:a + n], preferred_element_type=F32)
        m_new = jnp.maximum(m, jnp.max(s, axis=-1, keepdims=True))
        p = jnp.exp2(s - m_new).astype(BF16)
        acc = acc * jnp.exp2(m - m_new) + jnp.dot(p, v_ref[a:a + n, :], preferred_element_type=F32)
        m = m_new
    return acc[:, :HEAD_DIM] / acc[:, HEAD_DIM:HEAD_DIM + 1]


def _gqa_kernel(q_ref, kt_ref, v_ref, o_ref, *, tq, chunks):
    q = jnp.concatenate([q_ref[:, g * HEAD_DIM:(g + 1) * HEAD_DIM] for g in range(GQA_GROUP)], axis=0)
    q = (q * (1.0 / GQA_K_FP8_SCALE)).astype(FP8)
    o = _online_softmax_pv(q, kt_ref.at[0, 0], v_ref.at[0, 0], chunks)
    for g in range(GQA_GROUP):
        o_ref[:, g * HEAD_DIM:(g + 1) * HEAD_DIM] = o[g * tq:(g + 1) * tq].astype(BF16)


def _key_chunks(total, size):
    return tuple((a, min(size, total - a)) for a in range(0, total, size))


def gqa_attention(gq, kt_all, v_all, *, seq, n_batch):
    tq = TQ
    nq = seq // tq
    s_all = kt_all.shape[-1]
    vw = v_all.shape[-1]
    gw = GQA_GROUP * HEAD_DIM
    return pl.pallas_call(
        functools.partial(_gqa_kernel, tq=tq, chunks=_key_chunks(s_all, GQA_KEY_CHUNK)),
        grid=(n_batch, GQA_KV_HEADS, nq),
        in_specs=[pl.BlockSpec((tq, gw), lambda b, kv, i: (b * nq + i, kv)),
                  pl.BlockSpec((1, 1, HEAD_DIM, s_all), lambda b, kv, i: (b, kv, 0, 0)),
                  pl.BlockSpec((1, 1, s_all, vw), lambda b, kv, i: (b, kv, 0, 0))],
        out_specs=pl.BlockSpec((tq, gw), lambda b, kv, i: (b * nq + i, kv)),
        out_shape=jax.ShapeDtypeStruct((n_batch * seq, GQA_WIDTH), BF16),
        compiler_params=_cparams("arbitrary", "arbitrary", "arbitrary"),
        name="gqa_attention",
    )(gq, kt_all, v_all)


def _ctx_attn_kernel(naq_ref, nak_ref, nav_ref, gq_ref, gk_ref, gv_ref, ona_ref, ogqa_ref):
    for h in range(NA_HEADS):
        hs = slice(h * HEAD_DIM, (h + 1) * HEAD_DIM)
        ona_ref[:, hs] = _softmax_pv([_qk(naq_ref[:, hs], nak_ref[:, hs])], [nav_ref[:, hs]],
                                     exp=jnp.exp2).astype(BF16)
    for h in range(GQA_Q_HEADS):
        hs = slice(h * HEAD_DIM, (h + 1) * HEAD_DIM)
        kv = h // GQA_GROUP
        ks = slice(kv * HEAD_DIM, (kv + 1) * HEAD_DIM)
        ogqa_ref[:, hs] = _softmax_pv([_qk(gq_ref[:, hs], gk_ref[:, ks])], [gv_ref[:, ks]],
                                      exp=jnp.exp2).astype(BF16)


def ctx_attention(naq, nak, nav, gq, gk, gv, *, ctx, n_lat, n_batch):
    blk0 = n_lat // ctx
    wide = lambda: pl.BlockSpec((ctx, NA_WIDTH), lambda b: (blk0 + b, 0))
    narrow = lambda: pl.BlockSpec((ctx, GQA_KV_WIDTH), lambda b: (blk0 + b, 0))
    out = lambda: pl.BlockSpec((ctx, NA_WIDTH), lambda b: (b, 0))
    return pl.pallas_call(
        _ctx_attn_kernel,
        grid=(n_batch,),
        in_specs=[wide(), wide(), wide(), wide(), narrow(), narrow()],
        out_specs=[out(), out()],
        out_shape=[jax.ShapeDtypeStruct((n_batch * ctx, NA_WIDTH), BF16)] * 2,
        compiler_params=_cparams("arbitrary"),
        name="ctx_attention",
    )(naq, nak, nav, gq, gk, gv)


def _merge_kernel(xa_ref, xb_ref, gt_ref, conv_ref, prev_ref, next_ref, yna_a, yna_b, ygqa_a, ygqa_b, gate_ref,
                  cw_ref, wc_ref, wn_ref, wg_ref, wo_ref, o_ref, *, tm, d, seq, ctx, n_lat, n_first_x, n_first_y):
    i = pl.program_id(0)
    cw = CONV_WIDTH

    def u_of(ref, rows):
        return ref[rows, cw:2 * cw].astype(F32) * ref[rows, 2 * cw:3 * cw].astype(F32)

    u = u_of(conv_ref, slice(None))
    hp = u_of(prev_ref, slice(15, 16))
    hn = u_of(next_ref, slice(0, 1))
    row = lax.broadcasted_iota(jnp.int32, (tm, 1), 0)
    r = i * tm + row
    pos = jnp.where(r < n_lat, r % seq, r % ctx)
    length = jnp.where(r < n_lat, seq, ctx)
    u_prev = jnp.where(row == 0, hp, pltpu.roll(u, 1, 0))
    u_prev = jnp.where(pos == 0, 0.0, u_prev)
    u_next = jnp.where(row == tm - 1, hn, pltpu.roll(u, tm - 1, 0))
    u_next = jnp.where(pos == length - 1, 0.0, u_next)
    y_conv = conv_ref[:, 0:cw].astype(F32) * (u_prev * cw_ref[0:1] + u * cw_ref[1:2] + u_next * cw_ref[2:3])

    m = gate_ref[:, 0:d].astype(F32) * jnp.dot(y_conv.astype(BF16), wc_ref[...], preferred_element_type=F32)
    y_na = _two_part_read(yna_a, yna_b, n_first_y)
    y_gqa = _two_part_read(ygqa_a, ygqa_b, n_first_y)
    m += gate_ref[:, d:2 * d].astype(F32) * jnp.dot(y_na, wn_ref[...], preferred_element_type=F32)
    m += gate_ref[:, 2 * d:3 * d].astype(F32) * jnp.dot(y_gqa, wg_ref[...], preferred_element_type=F32)
    out = jnp.dot(m.astype(BF16), wo_ref[...], preferred_element_type=F32)
    o_ref[...] = _two_part_read(xa_ref, xb_ref, n_first_x) + gt_ref[0] * out


def merge_branches(x_parts, mods, conv, y_na_parts, y_gqa_parts, gates, conv_w, wc, wn, wg, wo, *, n_rows, seq,
                   ctx, n_lat, n_batch):
    tm = TM
    x_a, x_b, n_first_x, _ = _two_parts(x_parts, tm)
    yna_a, yna_b, n_first_y, _ = _two_parts(y_na_parts, tm)
    ygqa_a, ygqa_b, _, _ = _two_parts(y_gqa_parts, tm)
    d = x_a.shape[1]
    halo = 16
    n_halo_blocks = conv.shape[0] // halo
    per = tm // halo

    def mod_row(i):
        return jnp.minimum((i * tm) // seq, n_batch)

    tok = lambda w: pl.BlockSpec((tm, w), lambda i: (i, 0))
    const = lambda shape: pl.BlockSpec(shape, lambda i: tuple(0 for _ in shape))
    return pl.pallas_call(
        functools.partial(_merge_kernel, tm=tm, d=d, seq=seq, ctx=ctx, n_lat=n_lat,
                          n_first_x=n_first_x, n_first_y=n_first_y),
        grid=(n_rows // tm,),
        in_specs=_two_part_specs(tm, d, n_first_x) + [
                  pl.BlockSpec((1, 1, d), lambda i: (mod_row(i), 0, 2)),
                  tok(3 * CONV_WIDTH),
                  pl.BlockSpec((halo, 3 * CONV_WIDTH), lambda i: (jnp.maximum(i * per - 1, 0), 0)),
                  pl.BlockSpec((halo, 3 * CONV_WIDTH),
                               lambda i: (jnp.minimum((i + 1) * per, n_halo_blocks - 1), 0)),
                  *_two_part_specs(tm, NA_WIDTH, n_first_y), *_two_part_specs(tm, GQA_WIDTH, n_first_y),
                  tok(3 * d),
                  const((3, CONV_WIDTH)), const((CONV_WIDTH, d)), const((NA_WIDTH, d)),
                  const((GQA_WIDTH, d)), const((d, d))],
        out_specs=tok(d),
        out_shape=jax.ShapeDtypeStruct((n_rows, d), F32),
        compiler_params=_cparams("arbitrary"),
        name="merge_branches",
    )(x_a, x_b, mods, conv, conv, conv, yna_a, yna_b, ygqa_a, ygqa_b, gates, conv_w, wc, wn, wg, wo)


def _router_kernel(x_ref, sh_ref, sc_ref, g_ref, rw_ref, rb_ref, tri_ref,
                   h_o, idx_o, gate_o, rank_o, cnt_o, carry, *, n_exp):
    i = pl.program_id(0)

    @pl.when(i == 0)
    def _():
        carry[...] = jnp.zeros_like(carry)

    h = _rms(x_ref[...]) * g_ref[...]
    h = h * (1.0 + sc_ref[0]) + sh_ref[0]
    h_o[...] = h
    logits = _dot_bf16x3(h, rw_ref[...]) + rb_ref[...]
    lane = lax.broadcasted_iota(jnp.int32, logits.shape, 1)
    work = logits
    vals, idxs, hots = [], [], []
    for _ in range(TOP_K):
        m = jnp.max(work, axis=-1, keepdims=True)
        ik = jnp.min(jnp.where(work == m, lane, n_exp), axis=-1, keepdims=True)
        hot = lane == ik
        vals.append(m)
        idxs.append(ik)
        hots.append(hot)
        work = jnp.where(hot, -jnp.inf, work)
    es = [jnp.exp(v - vals[0]) for v in vals]
    den = functools.reduce(lambda a, b: a + b, es)
    hot_sum = functools.reduce(lambda a, b: a + b, [jnp.where(hh, 1.0, 0.0) for hh in hots])
    prefix = jnp.dot(tri_ref[...], hot_sum.astype(BF16), preferred_element_type=F32) + carry[...]
    ranks = [jnp.sum(jnp.where(hh, prefix, 0.0), axis=-1, keepdims=True) for hh in hots]
    carry[...] = carry[...] + jnp.sum(hot_sum, axis=0, keepdims=True)
    cnt_o[...] = carry[...].astype(jnp.int32)

    out_lane = lax.broadcasted_iota(jnp.int32, idx_o.shape, 1)

    def spread(cols, fill):
        acc = jnp.full(idx_o.shape, fill, cols[0].dtype)
        for k, c in enumerate(cols):
            acc = jnp.where(out_lane == k, c, acc)
        return acc

    idx_o[...] = spread(idxs, 0)
    gate_o[...] = spread([e / den for e in es], 0.0)
    rank_o[...] = spread([rk.astype(jnp.int32) for rk in ranks], 0)


def router(x_all, mods, g2, router_w, router_b, *, n_rows, seq, n_batch):
    d = x_all.shape[1]
    n_exp = router_w.shape[1]
    tm = TM
    tri = jnp.asarray(np.tril(np.ones((tm, tm), np.float32), -1), BF16)

    def mod_row(i):
        return jnp.minimum((i * tm) // seq, n_batch)

    tok = lambda w: pl.BlockSpec((tm, w), lambda i: (i, 0))
    const = lambda shape: pl.BlockSpec(shape, lambda i: tuple(0 for _ in shape))
    return pl.pallas_call(
        functools.partial(_router_kernel, n_exp=n_exp),
        grid=(n_rows // tm,),
        in_specs=[tok(d),
                  pl.BlockSpec((1, 1, d), lambda i: (mod_row(i), 0, 3)),
                  pl.BlockSpec((1, 1, d), lambda i: (mod_row(i), 0, 4)),
                  const((1, d)), const((d, n_exp)), const((1, n_exp)), const((tm, tm))],
        out_specs=[tok(d), tok(128), tok(128), tok(128), const((1, n_exp))],
        out_shape=[jax.ShapeDtypeStruct((n_rows, d), F32),
                   jax.ShapeDtypeStruct((n_rows, 128), jnp.int32),
                   jax.ShapeDtypeStruct((n_rows, 128), F32),
                   jax.ShapeDtypeStruct((n_rows, 128), jnp.int32),
                   jax.ShapeDtypeStruct((1, n_exp), jnp.int32)],
        scratch_shapes=[pltpu.VMEM((1, n_exp), F32)],
        compiler_params=_cparams("arbitrary"),
        name="router",
    )(x_all, mods, mods, g2, router_w, router_b.reshape(1, n_exp), tri)


def _dispatch_kernel(fill_lo_ref, fill_hi_ref, dest_ref, h_ref, xs_ref, sem, *, tg, n_exp):
    i = pl.program_id(0)

    def row_copy(t, slot):
        return pltpu.make_async_copy(h_ref.at[pl.ds(t, 1)], xs_ref.at[pl.ds(slot, 1)], sem)

    def issue(g, c):
        for j in range(ROW_UNROLL):
            t = g * ROW_UNROLL + j
            for k in range(TOP_K):
                row_copy(t, dest_ref[t * TOP_K + k]).start(priority=k % 2)
        return c

    lax.fori_loop(0, tg // ROW_UNROLL, issue, 0)

    def drain(g, c):
        for _ in range(ROW_UNROLL * TOP_K):
            row_copy(0, 0).wait()
        return c

    lax.fori_loop(0, tg // ROW_UNROLL, drain, 0)

    @pl.when(i == pl.num_programs(0) - 1)
    def _():
        for e in range(n_exp):
            lo = fill_lo_ref[e]
            hi = fill_hi_ref[e]

            def fill(p, c):
                row_copy(0, p).start()
                return c

            lax.fori_loop(lo, hi, fill, 0)

            def fill_wait(p, c):
                row_copy(0, 0).wait()
                return c

            lax.fori_loop(lo, hi, fill_wait, 0)

        def tail_copy(blk):
            return pltpu.make_async_copy(h_ref, xs_ref.at[pl.ds(pl.multiple_of(blk * tg, tg), tg)], sem)

        first_free = fill_hi_ref[n_exp - 1] // tg

        def tail(blk, c):
            tail_copy(blk).start()
            tail_copy(blk).wait()
            return c

        lax.fori_loop(first_free, xs_ref.shape[0] // tg, tail, 0)


def dispatch(h2, dest_flat, fill_lo, fill_hi, cap):
    n, d = h2.shape
    tg = TG
    assert BM % tg == 0 and cap % tg == 0
    n_exp = fill_lo.shape[0]
    return pl.pallas_call(
        functools.partial(_dispatch_kernel, tg=tg, n_exp=n_exp),
        grid_spec=pltpu.PrefetchScalarGridSpec(
            num_scalar_prefetch=2,
            grid=(n // tg,),
            in_specs=[pl.BlockSpec((tg * TOP_K,), lambda i, lo, hi: (i,), memory_space=pltpu.SMEM),
                      pl.BlockSpec((tg, d), lambda i, lo, hi: (i, 0))],
            out_specs=pl.BlockSpec(memory_space=pl.ANY),
            scratch_shapes=[pltpu.SemaphoreType.DMA(())]),
        out_shape=jax.ShapeDtypeStruct((cap, d), F32),
        compiler_params=_cparams("arbitrary"),
        name="moe_dispatch",
    )(fill_lo, fill_hi, dest_flat, h2)


def _gate_up_selector():
    sel = np.zeros((256, 256), np.float32)
    sel[2 * np.arange(128), np.arange(128)] = 1.0
    sel[2 * np.arange(128) + 1, 128 + np.arange(128)] = 1.0
    return jnp.asarray(sel, BF16)


def _expert_kernel(be_ref, nu_ref, x_ref, wgu_ref, bg_ref, bu_ref, wd_ref, bd_ref, sel_ref, y_ref,
                   wg_s, wu_s, wd_s):
    m = pl.program_id(0)
    active = m < nu_ref[0]
    new_expert = jnp.logical_or(m == 0, be_ref[m] != be_ref[jnp.maximum(m - 1, 0)])

    @pl.when(jnp.logical_and(active, new_expert))
    def _():
        for c in range(wgu_ref.shape[-1] // 256):
            t = jnp.dot(wgu_ref[0, :, c * 256:(c + 1) * 256].astype(BF16), sel_ref[...],
                        preferred_element_type=F32)
            wg_s[:, c * 128:(c + 1) * 128] = t[:, :128].astype(BF16)
            wu_s[:, c * 128:(c + 1) * 128] = t[:, 128:].astype(BF16)
        wd_s[...] = wd_ref[0].astype(BF16)

    @pl.when(active)
    def _():
        x = x_ref[...].astype(BF16)
        hg = jnp.dot(x, wg_s[...], preferred_element_type=F32) + bg_ref[0]
        hu = jnp.dot(x, wu_s[...], preferred_element_type=F32) + bu_ref[0]
        g = jnp.minimum(hg, SWIGLU_LIMIT)
        u = jnp.clip(hu, -SWIGLU_LIMIT, SWIGLU_LIMIT)
        a = (u + 1.0) * (g * jax.nn.sigmoid(SWIGLU_ALPHA * g))
        y_ref[...] = jnp.dot(a.astype(BF16), wd_s[...], preferred_element_type=F32) + bd_ref[0]

    @pl.when(jnp.logical_not(active))
    def _():
        y_ref[...] = jnp.zeros_like(y_ref)


def experts(xs, block_e, n_used, w_gu, bg, bu, w_down, bd):
    cap, d = xs.shape
    bm = BM
    _, _, f2 = w_gu.shape
    f = f2 // 2
    n_blocks = cap // bm

    def blk(m, be, nu):
        return jnp.minimum(m, nu[0] - 1)

    def exp_of(m, be, nu):
        return be[blk(m, be, nu)]

    per_expert = lambda shape: pl.BlockSpec(shape, lambda m, be, nu: (exp_of(m, be, nu), 0, 0))
    return pl.pallas_call(
        _expert_kernel,
        grid_spec=pltpu.PrefetchScalarGridSpec(
            num_scalar_prefetch=2,
            grid=(n_blocks,),
            in_specs=[pl.BlockSpec((bm, d), lambda m, be, nu: (blk(m, be, nu), 0)),
                      per_expert((1, d, f2)), per_expert((1, 1, f)), per_expert((1, 1, f)),
                      per_expert((1, f, d)), per_expert((1, 1, d)),
                      pl.BlockSpec((256, 256), lambda m, be, nu: (0, 0))],
            out_specs=pl.BlockSpec((bm, d), lambda m, be, nu: (m, 0)),
            scratch_shapes=[pltpu.VMEM((d, f), BF16), pltpu.VMEM((d, f), BF16), pltpu.VMEM((f, d), BF16)]),
        out_shape=jax.ShapeDtypeStruct((cap, d), F32),
        compiler_params=_cparams("arbitrary"),
        name="moe_experts",
    )(block_e, n_used, xs, w_gu, bg, bu, w_down, bd, _gate_up_selector())


def _combine_kernel(dest_ref, y_ref, x_ref, gt_ref, gate_ref, fg_ref, o_ref, buf, sem, *, tg, final_norm):
    def row_copy(slot, k, t):
        return pltpu.make_async_copy(y_ref.at[pl.ds(slot, 1)], buf.at[k, pl.ds(t, 1)], sem)

    def issue(g, c):
        for j in range(ROW_UNROLL):
            t = g * ROW_UNROLL + j
            for k in range(TOP_K):
                row_copy(dest_ref[t * TOP_K + k], k, t).start(priority=k % 2)
        return c

    lax.fori_loop(0, tg // ROW_UNROLL, issue, 0)

    def drain(g, c):
        for _ in range(ROW_UNROLL * TOP_K):
            row_copy(0, 0, 0).wait()
        return c

    lax.fori_loop(0, tg // ROW_UNROLL, drain, 0)

    acc = gate_ref[:, 0:1] * buf[0]
    for k in range(1, TOP_K):
        acc += gate_ref[:, k:k + 1] * buf[k]
    xn = x_ref[...] + gt_ref[0] * acc
    if final_norm:
        xn = _rms(xn) * fg_ref[...]
    o_ref[...] = xn


def combine(x_all, mods, y, dest_flat, gates, final_g, *, n_rows, seq, n_batch, final_norm):
    d = x_all.shape[1]
    tg = TG

    def mod_row(i):
        return jnp.minimum((i * tg) // seq, n_batch)

    return pl.pallas_call(
        functools.partial(_combine_kernel, tg=tg, final_norm=final_norm),
        grid=(n_rows // tg,),
        in_specs=[pl.BlockSpec((tg * TOP_K,), lambda i: (i,), memory_space=pltpu.SMEM),
                  pl.BlockSpec(memory_space=pl.ANY),
                  pl.BlockSpec((tg, d), lambda i: (i, 0)),
                  pl.BlockSpec((1, 1, d), lambda i: (mod_row(i), 0, 5)),
                  pl.BlockSpec((tg, 128), lambda i: (i, 0)),
                  pl.BlockSpec((1, d), lambda i: (0, 0))],
        out_specs=pl.BlockSpec((tg, d), lambda i: (i, 0)),
        out_shape=jax.ShapeDtypeStruct((n_rows, d), F32),
        scratch_shapes=[pltpu.VMEM((TOP_K, tg, d), F32), pltpu.SemaphoreType.DMA(())],
        compiler_params=_cparams("arbitrary"),
        name="moe_combine",
    )(dest_flat, y, x_all, mods, gates, final_g)


def _rope_tables(seq, tm):
    half = HEAD_DIM // 2
    inv = ROPE_THETA ** (-jnp.arange(0, half, 2, dtype=F32) / half)
    t = jnp.arange(seq)
    lane = np.arange(128)
    dd = lane % HEAD_DIM
    use_col = (dd // half) == 1
    j = dd % (half // 2)
    second = (dd % half) >= (half // 2)
    pos = jnp.where(jnp.asarray(use_col)[None, :], (t % GRID_W)[:, None], (t // GRID_W)[:, None]).astype(F32)
    ang = pos * inv[jnp.asarray(j)][None, :]
    cos = jnp.cos(ang)
    sin = jnp.sin(ang) * jnp.where(jnp.asarray(second), 1.0, -1.0)[None, :]
    cos = jnp.concatenate([cos, jnp.ones((tm, 128), F32)], axis=0)
    sin = jnp.concatenate([sin, jnp.zeros((tm, 128), F32)], axis=0)
    return cos, sin


def _moe_plan(idx, rank, counts, n_exp, n_rows):
    bm = BM
    nk = n_rows * TOP_K
    n_blocks = -(-nk // bm) + n_exp
    counts = counts.reshape(n_exp)
    padded = (counts + bm - 1) // bm * bm
    pend = jnp.cumsum(padded)
    pstart = pend - padded
    dest = (pstart[idx] + rank).reshape(nk).astype(jnp.int32)
    block_start = jnp.arange(n_blocks, dtype=jnp.int32) * bm
    block_e = jnp.minimum(jnp.sum(pend[None, :] <= block_start[:, None], axis=1), n_exp - 1).astype(jnp.int32)
    n_used = (pend[-1:] // bm).astype(jnp.int32)
    return dest, block_e, n_used, (pstart + counts).astype(jnp.int32), pend.astype(jnp.int32), n_blocks * bm


def kernel(x, c, ctx, c_ctx, ada_w, ada_b, norm1_g, norm2_g, w_in, conv_w, na_rpb, q_norm_g, k_norm_g,
           w_br_conv, w_br_na, w_br_gqa, w_out, router_w, router_b, w_gu, b_gu, w_down, b_down, final_g):
    n_batch, seq, d = x.shape
    n_ctx = ctx.shape[1]
    depth = ada_w.shape[0]
    n_exp = router_w.shape[-1]
    n_lat = n_batch * seq
    n_all = n_lat + n_batch * n_ctx
    assert n_batch + 1 <= MOD_ROWS and seq % TM == 0 and (n_batch * n_ctx) % TM == 0

    cond = jnp.zeros((MOD_ROWS, d), F32).at[:n_batch].set(c).at[n_batch].set(c_ctx)
    mods_all = ada_mods(cond, ada_w, ada_b)
    cos_t, sin_t = _rope_tables(seq, TM)
    hsum = jnp.asarray(np.kron(np.eye(256 // HEAD_DIM), np.ones((HEAD_DIM, HEAD_DIM))), BF16)

    f = w_gu.shape[-1] // 2
    w_gu_flat = w_gu.reshape(depth * n_exp, d, 2 * f)
    bg = b_gu[..., 0::2].reshape(depth * n_exp, 1, f)
    bu = b_gu[..., 1::2].reshape(depth * n_exp, 1, f)
    w_down_flat = w_down.reshape(depth * n_exp, f, d)
    bd = b_down.reshape(depth * n_exp, 1, d)

    x_parts = (x.reshape(n_lat, d), ctx.reshape(n_batch * n_ctx, d))
    for l in range(depth):
        last = l == depth - 1
        mods = mods_all[l].reshape(MOD_ROWS, 1, 6 * d)
        qg = jnp.tile(q_norm_g[l], 256 // HEAD_DIM).reshape(1, 256)
        kg = jnp.tile(k_norm_g[l], 128 // HEAD_DIM).reshape(1, 128)
        conv, naq, nak, nav, gq, gk, gv, gates = in_projection(
            x_parts, mods, norm1_g[l].reshape(1, d), w_in[l].astype(BF16), cos_t, sin_t, qg, kg, hsum,
            seq=seq, n_lat=n_lat, n_batch=n_batch)

        y_na = neighborhood_attention(naq, nak, nav, na_bias_table(na_rpb[l] * LOG2E, seq),
                                      seq=seq, ctx=n_ctx, n_batch=n_batch)

        def kv_all(t):
            return jnp.concatenate([t[:n_lat].reshape(n_batch, seq, GQA_KV_WIDTH),
                                    t[n_lat:].reshape(n_batch, n_ctx, GQA_KV_WIDTH)], axis=1)

        s_all = seq + n_ctx
        kt_all = (kv_all(gk).transpose(0, 2, 1) * GQA_K_FP8_SCALE).astype(FP8)
        kt_all = kt_all.reshape(n_batch, GQA_KV_HEADS, HEAD_DIM, s_all)
        gv_all = kv_all(gv)
        v_tail = jnp.concatenate([jnp.ones((n_batch, s_all, 1), BF16),
                                  jnp.zeros((n_batch, s_all, HEAD_DIM - 1), BF16)], axis=-1)
        v_all = jnp.stack([jnp.concatenate([gv_all[..., kv * HEAD_DIM:(kv + 1) * HEAD_DIM], v_tail], axis=-1)
                           for kv in range(GQA_KV_HEADS)], axis=1)
        y_gqa = gqa_attention(gq, kt_all, v_all, seq=seq, n_batch=n_batch)

        n_rows = n_lat if last else n_all
        y_na_parts, y_gqa_parts = (y_na,), (y_gqa,)
        if not last:
            yc_na, yc_gqa = ctx_attention(naq, nak, nav, gq, gk, gv, ctx=n_ctx, n_lat=n_lat, n_batch=n_batch)
            y_na_parts, y_gqa_parts = (y_na, yc_na), (y_gqa, yc_gqa)
        x_all = merge_branches(x_parts, mods, conv, y_na_parts, y_gqa_parts, gates, conv_w[l],
                               w_br_conv[l].astype(BF16), w_br_na[l].astype(BF16), w_br_gqa[l].astype(BF16),
                               w_out[l].astype(BF16), n_rows=n_rows, seq=seq, ctx=n_ctx, n_lat=n_lat,
                               n_batch=n_batch)

        h2, idx, gate_w, rank, counts = router(x_all, mods, norm2_g[l].reshape(1, d), router_w[l], router_b[l],
                                               n_rows=n_rows, seq=seq, n_batch=n_batch)
        dest, block_e, n_used, fill_lo, fill_hi, cap = _moe_plan(
            idx[:, :TOP_K], rank[:, :TOP_K], counts, n_exp, n_rows)
        xs = dispatch(h2, dest, fill_lo, fill_hi, cap)
        y = experts(xs, block_e + l * n_exp, n_used, w_gu_flat, bg, bu, w_down_flat, bd)
        x_all = combine(x_all, mods, y, dest, gate_w, final_g.reshape(1, d),
                        n_rows=n_rows, seq=seq, n_batch=n_batch, final_norm=last)
        x_parts = (x_all,)
    return x_all.reshape(n_batch, seq, d)
```

```python
import functools

import jax
import jax.numpy as jnp
import numpy as np
from jax import lax
from jax.experimental import pallas as pl
from jax.experimental.pallas import tpu as pltpu

F32 = jnp.float32
BF16 = jnp.bfloat16

GRID_W = 64
EPS = 1e-6
HEAD_DIM = 64
CONV_WIDTH = 512
NA_HEADS = 8
NA_WIDTH = NA_HEADS * HEAD_DIM
NA_KH = 8
NA_KW = 16
NA_QROWS = 4
NA_BAND = NA_QROWS + NA_KH
GQA_Q_HEADS = 8
GQA_KV_HEADS = 2
GQA_GROUP = GQA_Q_HEADS // GQA_KV_HEADS
GQA_WIDTH = GQA_Q_HEADS * HEAD_DIM
GQA_KV_WIDTH = GQA_KV_HEADS * HEAD_DIM
ROPE_THETA = 10000.0
TOP_K = 4
SWIGLU_ALPHA = 1.702
SWIGLU_LIMIT = 7.0
LOG2E = float(np.log2(np.e))
ATTN_Q_SCALE = HEAD_DIM ** -0.5 * LOG2E
GQA_Q_SCALE = ATTN_Q_SCALE
FP8 = jnp.float8_e4m3fn
GQA_K_FP8_SCALE = 0.125
MOD_ROWS = 16
NEG_BIG = -1e30
VMEM_LIMIT = 56 * 1024 * 1024

TM = 512
TQ = 256
GQA_KEY_CHUNK = 2048
BM = 512
TG = 256
ROW_UNROLL = 8


def _cparams(*sem):
    return pltpu.CompilerParams(dimension_semantics=sem, vmem_limit_bytes=VMEM_LIMIT)


def _rms(x):
    return x * lax.rsqrt(jnp.mean(x * x, axis=-1, keepdims=True) + EPS)


def _dot_bf16x3(a, b):
    a_hi = a.astype(BF16)
    b_hi = b.astype(BF16)
    a_lo = (a - a_hi.astype(F32)).astype(BF16)
    b_lo = (b - b_hi.astype(F32)).astype(BF16)
    dot = functools.partial(jnp.dot, preferred_element_type=F32)
    return dot(a_hi, b_hi) + (dot(a_lo, b_hi) + dot(a_hi, b_lo))


def _two_part_specs(tm, width, n_first):
    return [pl.BlockSpec((tm, width), lambda i: (jnp.minimum(i, n_first - 1), 0)),
            pl.BlockSpec((tm, width), lambda i: (jnp.maximum(i - n_first, 0), 0))]


def _two_part_read(a_ref, b_ref, n_first):
    return jnp.where(pl.program_id(0) < n_first, a_ref[...], b_ref[...])


def _two_parts(parts, tm):
    if len(parts) == 1:
        return parts[0], parts[0], parts[0].shape[0] // tm, parts[0].shape[0]
    return parts[0], parts[1], parts[0].shape[0] // tm, parts[0].shape[0] + parts[1].shape[0]


def _ada_kernel(c_ref, w_ref, b_ref, o_ref):
    c = c_ref[...]
    s = c * jax.nn.sigmoid(c)
    o_ref[0] = jnp.dot(s, w_ref[0], precision=lax.Precision.HIGHEST,
                       preferred_element_type=F32) + b_ref[0]


def ada_mods(cond, ada_w, ada_b):
    depth, d, n6 = ada_w.shape
    tn = d
    return pl.pallas_call(
        _ada_kernel,
        grid=(depth, n6 // tn),
        in_specs=[pl.BlockSpec((MOD_ROWS, d), lambda l, j: (0, 0)),
                  pl.BlockSpec((1, d, tn), lambda l, j: (l, 0, j)),
                  pl.BlockSpec((1, 1, tn), lambda l, j: (l, 0, j))],
        out_specs=pl.BlockSpec((1, MOD_ROWS, tn), lambda l, j: (l, 0, j)),
        out_shape=jax.ShapeDtypeStruct((depth, MOD_ROWS, n6), F32),
        compiler_params=_cparams("arbitrary", "arbitrary"),
        name="ada_mods",
    )(cond, ada_w, ada_b.reshape(depth, 1, n6))


def _head_inv_rms(t, hsum_ref):
    tot = jnp.dot((t * t).astype(BF16), hsum_ref[...], preferred_element_type=F32)
    return lax.rsqrt(tot * (1.0 / HEAD_DIM) + EPS)


def _rope(t, cos, sin):
    lane = lax.broadcasted_iota(jnp.int32, t.shape, 1)
    second = (lane % 32) >= 16
    partner = jnp.where(second, pltpu.roll(t, 16, 1), pltpu.roll(t, 128 - 16, 1))
    return t * cos + partner * sin


def _inproj_kernel(xa_ref, xb_ref, sh_ref, sc_ref, g_ref, w_ref, cos_ref, sin_ref, qg_ref, kg_ref, hsum_ref,
                   conv_o, naq_o, nak_o, nav_o, gq_o, gk_o, gv_o, gate_o, h_scr, *, d, n_first):
    h = _rms(_two_part_read(xa_ref, xb_ref, n_first)) * g_ref[...]
    h = h * (1.0 + sc_ref[0]) + sh_ref[0]
    h_scr[...] = h.astype(BF16)

    def proj(a, n):
        return jnp.dot(h_scr[...], w_ref[:, a:a + n], preferred_element_type=F32)

    off = 0
    for j in range(3):
        conv_o[:, j * CONV_WIDTH:(j + 1) * CONV_WIDTH] = proj(off, CONV_WIDTH).astype(BF16)
        off += CONV_WIDTH
    naq_o[...] = (proj(off, NA_WIDTH) * ATTN_Q_SCALE).astype(BF16)
    off += NA_WIDTH
    nak_o[...] = proj(off, NA_WIDTH).astype(BF16)
    off += NA_WIDTH
    nav_o[...] = proj(off, NA_WIDTH).astype(BF16)
    off += NA_WIDTH
    cos = cos_ref[...]
    sin = sin_ref[...]
    for j in range(GQA_WIDTH // 256):
        t = proj(off + j * 256, 256)
        t = t * _head_inv_rms(t, hsum_ref) * qg_ref[...]
        for half in range(2):
            lanes = slice(half * 128, (half + 1) * 128)
            gq_o[:, j * 256 + half * 128:j * 256 + (half + 1) * 128] = (
                _rope(t[:, lanes], cos, sin) * GQA_Q_SCALE).astype(BF16)
    off += GQA_WIDTH
    t = proj(off, 2 * GQA_KV_WIDTH)
    inv = _head_inv_rms(t, hsum_ref)
    gk_o[...] = _rope(t[:, :GQA_KV_WIDTH] * inv[:, :GQA_KV_WIDTH] * kg_ref[...], cos, sin).astype(BF16)
    gv_o[...] = t[:, GQA_KV_WIDTH:].astype(BF16)
    off += 2 * GQA_KV_WIDTH
    for j in range(3):
        gate_o[:, j * d:(j + 1) * d] = jax.nn.sigmoid(proj(off + j * d, d)).astype(BF16)


def in_projection(x_parts, mods, g1, w_in_bf, cos_t, sin_t, qg, kg, hsum, *, seq, n_lat, n_batch):
    tm = TM
    x_a, x_b, n_first, nt = _two_parts(x_parts, tm)
    d = x_a.shape[1]
    n_lat_tiles = n_lat // tm
    seq_tiles = seq // tm
    in_cols = w_in_bf.shape[1]

    def mod_row(i):
        return jnp.minimum((i * tm) // seq, n_batch)

    def rope_blk(i):
        return jnp.where(i < n_lat_tiles, i % seq_tiles, seq_tiles)

    tok = lambda w: pl.BlockSpec((tm, w), lambda i: (i, 0))
    const = lambda shape: pl.BlockSpec(shape, lambda i: tuple(0 for _ in shape))
    outs = [CONV_WIDTH * 3, NA_WIDTH, NA_WIDTH, NA_WIDTH, GQA_WIDTH, GQA_KV_WIDTH, GQA_KV_WIDTH, 3 * d]
    return pl.pallas_call(
        functools.partial(_inproj_kernel, d=d, n_first=n_first),
        grid=(nt // tm,),
        in_specs=_two_part_specs(tm, d, n_first) + [
                  pl.BlockSpec((1, 1, d), lambda i: (mod_row(i), 0, 0)),
                  pl.BlockSpec((1, 1, d), lambda i: (mod_row(i), 0, 1)),
                  const((1, d)),
                  pl.BlockSpec((d, in_cols), lambda i: (0, 0), pipeline_mode=pl.Buffered(1)),
                  pl.BlockSpec((tm, 128), lambda i: (rope_blk(i), 0)),
                  pl.BlockSpec((tm, 128), lambda i: (rope_blk(i), 0)),
                  const((1, 256)), const((1, 128)), const((256, 256))],
        out_specs=[tok(w) for w in outs],
        out_shape=[jax.ShapeDtypeStruct((nt, w), BF16) for w in outs],
        scratch_shapes=[pltpu.VMEM((tm, d), BF16)],
        compiler_params=_cparams("arbitrary"),
        name="in_projection",
    )(x_a, x_b, mods, mods, g1, w_in_bf, cos_t, sin_t, qg, kg, hsum)


def _softmax_pv(scores, values, exp=jnp.exp):
    m = functools.reduce(jnp.maximum, [jnp.max(s, axis=-1, keepdims=True) for s in scores])
    ps = [exp(s - m) for s in scores]
    l = functools.reduce(lambda a, b: a + b, [jnp.sum(p, axis=-1, keepdims=True) for p in ps])
    o = functools.reduce(lambda a, b: a + b,
                         [jnp.dot(p.astype(BF16), v, preferred_element_type=F32) for p, v in zip(ps, values)])
    return o / l


def _qk(q, k):
    return lax.dot_general(q, k, (((1,), (1,)), ((), ())), preferred_element_type=F32)


def _na_kernel(q_ref, k_ref, v_ref, kc_ref, vc_ref, bias_ref, o_ref, *, w, rows):
    j = pl.program_id(1)
    start = jnp.clip(j * NA_QROWS - NA_KH // 2, 0, rows - NA_BAND) * w
    start = pl.multiple_of(start, w)
    nkeys = NA_BAND * w
    first_head = lax.broadcasted_iota(jnp.int32, (1, 2 * HEAD_DIM), 1) < HEAD_DIM
    for pair in range(NA_HEADS // 2):
        ls = slice(pair * 2 * HEAD_DIM, (pair + 1) * 2 * HEAD_DIM)
        q2 = q_ref[:, ls]
        kw = k_ref[pl.ds(start, nkeys), ls]
        vw = v_ref[pl.ds(start, nkeys), ls]
        kc = kc_ref[:, ls]
        vc = vc_ref[:, ls]
        outs = []
        for e in range(2):
            q = jnp.where(first_head == (e == 0), q2, jnp.zeros_like(q2))
            s_win = _qk(q, kw) + bias_ref[0, 2 * pair + e]
            s_ctx = _qk(q, kc)
            outs.append(_softmax_pv([s_win, s_ctx], [vw, vc], exp=jnp.exp2))
        o_ref[:, ls] = jnp.where(first_head, outs[0], outs[1]).astype(BF16)


def neighborhood_attention(naq, nak, nav, bias, *, seq, ctx, n_batch):
    w = GRID_W
    rows = seq // w
    nj = rows // NA_QROWS
    nq = NA_QROWS * w
    n_lat = n_batch * seq
    ctx_blk0 = n_lat // ctx

    def case(j):
        return jnp.where(j == 0, 0, jnp.where(j == nj - 1, 2, 1))

    return pl.pallas_call(
        functools.partial(_na_kernel, w=w, rows=rows),
        grid=(n_batch, nj),
        in_specs=[pl.BlockSpec((nq, NA_WIDTH), lambda b, j: (b * nj + j, 0)),
                  pl.BlockSpec((seq, NA_WIDTH), lambda b, j: (b, 0)),
                  pl.BlockSpec((seq, NA_WIDTH), lambda b, j: (b, 0)),
                  pl.BlockSpec((ctx, NA_WIDTH), lambda b, j: (ctx_blk0 + b, 0)),
                  pl.BlockSpec((ctx, NA_WIDTH), lambda b, j: (ctx_blk0 + b, 0)),
                  pl.BlockSpec((1, NA_HEADS, nq, NA_BAND * w), lambda b, j: (case(j), 0, 0, 0))],
        out_specs=pl.BlockSpec((nq, NA_WIDTH), lambda b, j: (b * nj + j, 0)),
        out_shape=jax.ShapeDtypeStruct((n_lat, NA_WIDTH), BF16),
        compiler_params=_cparams("arbitrary", "arbitrary"),
        name="neighborhood_attention",
    )(naq, nak, nav, nak, nav, bias)


def na_bias_table(rpb, seq):
    w = GRID_W
    rows = seq // w
    nj = rows // NA_QROWS
    n_heads = rpb.shape[0]
    cols = np.arange(w)
    col_start = np.clip(cols - NA_KW // 2, 0, w - NA_KW)
    col_ok = (cols[None, :] >= col_start[:, None]) & (cols[None, :] < col_start[:, None] + NA_KW)
    pad = max(0, w - NA_KW)
    rp = jnp.pad(rpb.astype(F32), ((0, 0), (0, 0), (pad, pad)))
    tcol = jnp.stack([rp[:, :, pad + NA_KW - 1 - c: pad + NA_KW - 1 - c + w] for c in range(w)], axis=2)
    tcol = jnp.where(jnp.asarray(col_ok)[None, None], tcol, NEG_BIG)
    masked = jnp.full((n_heads, w, w), NEG_BIG, F32)
    tabs = []
    for j in (0, 1, nj - 1):
        start = int(np.clip(j * NA_QROWS - NA_KH // 2, 0, rows - NA_BAND))
        q_rows = []
        for a in range(NA_QROWS):
            r = j * NA_QROWS + a
            rs = int(np.clip(r - NA_KH // 2, 0, rows - NA_KH))
            blocks = []
            for i in range(NA_BAND):
                kr = start + i
                blocks.append(tcol[:, kr - r + NA_KH - 1] if rs <= kr < rs + NA_KH else masked)
            q_rows.append(jnp.concatenate(blocks, axis=-1))
        tabs.append(jnp.concatenate(q_rows, axis=1))
    return jnp.stack(tabs)


def _online_softmax_pv(q, kt_ref, v_ref, chunks):
    m = jnp.full((q.shape[0], 1), -jnp.inf, F32)
    acc = jnp.zeros((q.shape[0], v_ref.shape[-1]), F32)
    for a, n in chunks:
        s = jnp.dot(q, kt_ref[:, a:a + n], preferred_element_type=F32)
        m_new = jnp.maximum(m, jnp.max(s, axis=-1, keepdims=True))
        p = jnp.exp2(s - m_new).astype(BF16)
        acc = acc * jnp.exp2(m - m_new) + jnp.dot(p, v_ref[a:a + n, :], preferred_element_type=F32)
        m = m_new
    return acc[:, :HEAD_DIM] / acc[:, HEAD_DIM:HEAD_DIM + 1]


def _gqa_kernel(q_ref, kt_ref, v_ref, o_ref, *, tq, chunks):
    q = jnp.concatenate([q_ref[:, g * HEAD_DIM:(g + 1) * HEAD_DIM] for g in range(GQA_GROUP)], axis=0)
    q = (q * (1.0 / GQA_K_FP8_SCALE)).astype(FP8)
    o = _online_softmax_pv(q, kt_ref.at[0, 0], v_ref.at[0, 0], chunks)
    for g in range(GQA_GROUP):
        o_ref[:, g * HEAD_DIM:(g + 1) * HEAD_DIM] = o[g * tq:(g + 1) * tq].astype(BF16)


def _key_chunks(total, size):
    return tuple((a, min(size, total - a)) for a in range(0, total, size))


def gqa_attention(gq, kt_all, v_all, *, seq, n_batch):
    tq = TQ
    nq = seq // tq
    s_all = kt_all.shape[-1]
    vw = v_all.shape[-1]
    gw = GQA_GROUP * HEAD_DIM
    return pl.pallas_call(
        functools.partial(_gqa_kernel, tq=tq, chunks=_key_chunks(s_all, GQA_KEY_CHUNK)),
        grid=(n_batch, GQA_KV_HEADS, nq),
        in_specs=[pl.BlockSpec((tq, gw), lambda b, kv, i: (b * nq + i, kv)),
                  pl.BlockSpec((1, 1, HEAD_DIM, s_all), lambda b, kv, i: (b, kv, 0, 0)),
                  pl.BlockSpec((1, 1, s_all, vw), lambda b, kv, i: (b, kv, 0, 0))],
        out_specs=pl.BlockSpec((tq, gw), lambda b, kv, i: (b * nq + i, kv)),
        out_shape=jax.ShapeDtypeStruct((n_batch * seq, GQA_WIDTH), BF16),
        compiler_params=_cparams("arbitrary", "arbitrary", "arbitrary"),
        name="gqa_attention",
    )(gq, kt_all, v_all)


def _ctx_attn_kernel(naq_ref, nak_ref, nav_ref, gq_ref, gk_ref, gv_ref, ona_ref, ogqa_ref):
    for h in range(NA_HEADS):
        hs = slice(h * HEAD_DIM, (h + 1) * HEAD_DIM)
        ona_ref[:, hs] = _softmax_pv([_qk(naq_ref[:, hs], nak_ref[:, hs])], [nav_ref[:, hs]],
                                     exp=jnp.exp2).astype(BF16)
    for h in range(GQA_Q_HEADS):
        hs = slice(h * HEAD_DIM, (h + 1) * HEAD_DIM)
        kv = h // GQA_GROUP
        ks = slice(kv * HEAD_DIM, (kv + 1) * HEAD_DIM)
        ogqa_ref[:, hs] = _softmax_pv([_qk(gq_ref[:, hs], gk_ref[:, ks])], [gv_ref[:, ks]],
                                      exp=jnp.exp2).astype(BF16)


def ctx_attention(naq, nak, nav, gq, gk, gv, *, ctx, n_lat, n_batch):
    blk0 = n_lat // ctx
    wide = lambda: pl.BlockSpec((ctx, NA_WIDTH), lambda b: (blk0 + b, 0))
    narrow = lambda: pl.BlockSpec((ctx, GQA_KV_WIDTH), lambda b: (blk0 + b, 0))
    out = lambda: pl.BlockSpec((ctx, NA_WIDTH), lambda b: (b, 0))
    return pl.pallas_call(
        _ctx_attn_kernel,
        grid=(n_batch,),
        in_specs=[wide(), wide(), wide(), wide(), narrow(), narrow()],
        out_specs=[out(), out()],
        out_shape=[jax.ShapeDtypeStruct((n_batch * ctx, NA_WIDTH), BF16)] * 2,
        compiler_params=_cparams("arbitrary"),
        name="ctx_attention",
    )(naq, nak, nav, gq, gk, gv)


def _merge_kernel(xa_ref, xb_ref, gt_ref, conv_ref, prev_ref, next_ref, yna_a, yna_b, ygqa_a, ygqa_b, gate_ref,
                  cw_ref, wc_ref, wn_ref, wg_ref, wo_ref, o_ref, *, tm, d, seq, ctx, n_lat, n_first_x, n_first_y):
    i = pl.program_id(0)
    cw = CONV_WIDTH

    def u_of(ref, rows):
        return ref[rows, cw:2 * cw].astype(F32) * ref[rows, 2 * cw:3 * cw].astype(F32)

    u = u_of(conv_ref, slice(None))
    hp = u_of(prev_ref, slice(15, 16))
    hn = u_of(next_ref, slice(0, 1))
    row = lax.broadcasted_iota(jnp.int32, (tm, 1), 0)
    r = i * tm + row
    pos = jnp.where(r < n_lat, r % seq, r % ctx)
    length = jnp.where(r < n_lat, seq, ctx)
    u_prev = jnp.where(row == 0, hp, pltpu.roll(u, 1, 0))
    u_prev = jnp.where(pos == 0, 0.0, u_prev)
    u_next = jnp.where(row == tm - 1, hn, pltpu.roll(u, tm - 1, 0))
    u_next = jnp.where(pos == length - 1, 0.0, u_next)
    y_conv = conv_ref[:, 0:cw].astype(F32) * (u_prev * cw_ref[0:1] + u * cw_ref[1:2] + u_next * cw_ref[2:3])

    m = gate_ref[:, 0:d].astype(F32) * jnp.dot(y_conv.astype(BF16), wc_ref[...], preferred_element_type=F32)
    y_na = _two_part_read(yna_a, yna_b, n_first_y)
    y_gqa = _two_part_read(ygqa_a, ygqa_b, n_first_y)
    m += gate_ref[:, d:2 * d].astype(F32) * jnp.dot(y_na, wn_ref[...], preferred_element_type=F32)
    m += gate_ref[:, 2 * d:3 * d].astype(F32) * jnp.dot(y_gqa, wg_ref[...], preferred_element_type=F32)
    out = jnp.dot(m.astype(BF16), wo_ref[...], preferred_element_type=F32)
    o_ref[...] = _two_part_read(xa_ref, xb_ref, n_first_x) + gt_ref[0] * out


def merge_branches(x_parts, mods, conv, y_na_parts, y_gqa_parts, gates, conv_w, wc, wn, wg, wo, *, n_rows, seq,
                   ctx, n_lat, n_batch):
    tm = TM
    x_a, x_b, n_first_x, _ = _two_parts(x_parts, tm)
    yna_a, yna_b, n_first_y, _ = _two_parts(y_na_parts, tm)
    ygqa_a, ygqa_b, _, _ = _two_parts(y_gqa_parts, tm)
    d = x_a.shape[1]
    halo = 16
    n_halo_blocks = conv.shape[0] // halo
    per = tm // halo

    def mod_row(i):
        return jnp.minimum((i * tm) // seq, n_batch)

    tok = lambda w: pl.BlockSpec((tm, w), lambda i: (i, 0))
    const = lambda shape: pl.BlockSpec(shape, lambda i: tuple(0 for _ in shape))
    return pl.pallas_call(
        functools.partial(_merge_kernel, tm=tm, d=d, seq=seq, ctx=ctx, n_lat=n_lat,
                          n_first_x=n_first_x, n_first_y=n_first_y),
        grid=(n_rows // tm,),
        in_specs=_two_part_specs(tm, d, n_first_x) + [
                  pl.BlockSpec((1, 1, d), lambda i: (mod_row(i), 0, 2)),
                  tok(3 * CONV_WIDTH),
                  pl.BlockSpec((halo, 3 * CONV_WIDTH), lambda i: (jnp.maximum(i * per - 1, 0), 0)),
                  pl.BlockSpec((halo, 3 * CONV_WIDTH),
                               lambda i: (jnp.minimum((i + 1) * per, n_halo_blocks - 1), 0)),
                  *_two_part_specs(tm, NA_WIDTH, n_first_y), *_two_part_specs(tm, GQA_WIDTH, n_first_y),
                  tok(3 * d),
                  const((3, CONV_WIDTH)), const((CONV_WIDTH, d)), const((NA_WIDTH, d)),
                  const((GQA_WIDTH, d)), const((d, d))],
        out_specs=tok(d),
        out_shape=jax.ShapeDtypeStruct((n_rows, d), F32),
        compiler_params=_cparams("arbitrary"),
        name="merge_branches",
    )(x_a, x_b, mods, conv, conv, conv, yna_a, yna_b, ygqa_a, ygqa_b, gates, conv_w, wc, wn, wg, wo)


def _router_kernel(x_ref, sh_ref, sc_ref, g_ref, rw_ref, rb_ref, tri_ref,
                   h_o, idx_o, gate_o, rank_o, cnt_o, carry, *, n_exp):
    i = pl.program_id(0)

    @pl.when(i == 0)
    def _():
        carry[...] = jnp.zeros_like(carry)

    h = _rms(x_ref[...]) * g_ref[...]
    h = h * (1.0 + sc_ref[0]) + sh_ref[0]
    h_o[...] = h
    logits = _dot_bf16x3(h, rw_ref[...]) + rb_ref[...]
    lane = lax.broadcasted_iota(jnp.int32, logits.shape, 1)
    work = logits
    vals, idxs, hots = [], [], []
    for _ in range(TOP_K):
        m = jnp.max(work, axis=-1, keepdims=True)
        ik = jnp.min(jnp.where(work == m, lane, n_exp), axis=-1, keepdims=True)
        hot = lane == ik
        vals.append(m)
        idxs.append(ik)
        hots.append(hot)
        work = jnp.where(hot, -jnp.inf, work)
    es = [jnp.exp(v - vals[0]) for v in vals]
    den = functools.reduce(lambda a, b: a + b, es)
    hot_sum = functools.reduce(lambda a, b: a + b, [jnp.where(hh, 1.0, 0.0) for hh in hots])
    prefix = jnp.dot(tri_ref[...], hot_sum.astype(BF16), preferred_element_type=F32) + carry[...]
    ranks = [jnp.sum(jnp.where(hh, prefix, 0.0), axis=-1, keepdims=True) for hh in hots]
    carry[...] = carry[...] + jnp.sum(hot_sum, axis=0, keepdims=True)
    cnt_o[...] = carry[...].astype(jnp.int32)

    out_lane = lax.broadcasted_iota(jnp.int32, idx_o.shape, 1)

    def spread(cols, fill):
        acc = jnp.full(idx_o.shape, fill, cols[0].dtype)
        for k, c in enumerate(cols):
            acc = jnp.where(out_lane == k, c, acc)
        return acc

    idx_o[...] = spread(idxs, 0)
    gate_o[...] = spread([e / den for e in es], 0.0)
    rank_o[...] = spread([rk.astype(jnp.int32) for rk in ranks], 0)


def router(x_all, mods, g2, router_w, router_b, *, n_rows, seq, n_batch):
    d = x_all.shape[1]
    n_exp = router_w.shape[1]
    tm = TM
    tri = jnp.asarray(np.tril(np.ones((tm, tm), np.float32), -1), BF16)

    def mod_row(i):
        return jnp.minimum((i * tm) // seq, n_batch)

    tok = lambda w: pl.BlockSpec((tm, w), lambda i: (i, 0))
    const = lambda shape: pl.BlockSpec(shape, lambda i: tuple(0 for _ in shape))
    return pl.pallas_call(
        functools.partial(_router_kernel, n_exp=n_exp),
        grid=(n_rows // tm,),
        in_specs=[tok(d),
                  pl.BlockSpec((1, 1, d), lambda i: (mod_row(i), 0, 3)),
                  pl.BlockSpec((1, 1, d), lambda i: (mod_row(i), 0, 4)),
                  const((1, d)), const((d, n_exp)), const((1, n_exp)), const((tm, tm))],
        out_specs=[tok(d), tok(128), tok(128), tok(128), const((1, n_exp))],
        out_shape=[jax.ShapeDtypeStruct((n_rows, d), F32),
                   jax.ShapeDtypeStruct((n_rows, 128), jnp.int32),
                   jax.ShapeDtypeStruct((n_rows, 128), F32),
                   jax.ShapeDtypeStruct((n_rows, 128), jnp.int32),
                   jax.ShapeDtypeStruct((1, n_exp), jnp.int32)],
        scratch_shapes=[pltpu.VMEM((1, n_exp), F32)],
        compiler_params=_cparams("arbitrary"),
        name="router",
    )(x_all, mods, mods, g2, router_w, router_b.reshape(1, n_exp), tri)


def _dispatch_kernel(fill_lo_ref, fill_hi_ref, dest_ref, h_ref, xs_ref, hbuf, sems, *, tg, n_exp):
    i = pl.program_id(0)
    last = pl.num_programs(0) - 1
    cur = i % 2

    hbuf[cur] = h_ref[...]

    def row_copy(which, t, slot):
        return pltpu.make_async_copy(hbuf.at[which, pl.ds(t, 1)], xs_ref.at[pl.ds(slot, 1)], sems.at[which])

    def issue(g, c):
        for j in range(ROW_UNROLL):
            t = g * ROW_UNROLL + j
            for k in range(TOP_K):
                row_copy(cur, t, dest_ref[t * TOP_K + k]).start(priority=k % 2)
        return c

    lax.fori_loop(0, tg // ROW_UNROLL, issue, 0)

    def drain(which):
        def body(g, c):
            for _ in range(ROW_UNROLL * TOP_K):
                row_copy(which, 0, 0).wait()
            return c

        lax.fori_loop(0, tg // ROW_UNROLL, body, 0)

    @pl.when(i > 0)
    def _():
        drain(1 - cur)

    @pl.when(i == last)
    def _():
        drain(cur)
        for e in range(n_exp):
            lo = fill_lo_ref[e]
            hi = fill_hi_ref[e]

            def fill(p, c):
                row_copy(cur, 0, p).start()
                return c

            lax.fori_loop(lo, hi, fill, 0)

            def fill_wait(p, c):
                row_copy(cur, 0, 0).wait()
                return c

            lax.fori_loop(lo, hi, fill_wait, 0)

        def tail_copy(blk):
            return pltpu.make_async_copy(hbuf.at[cur], xs_ref.at[pl.ds(pl.multiple_of(blk * tg, tg), tg)],
                                         sems.at[cur])

        first_free = fill_hi_ref[n_exp - 1] // tg

        def tail(blk, c):
            tail_copy(blk).start()
            tail_copy(blk).wait()
            return c

        lax.fori_loop(first_free, xs_ref.shape[0] // tg, tail, 0)


def dispatch(h2, dest_flat, fill_lo, fill_hi, cap):
    n, d = h2.shape
    tg = TG
    assert BM % tg == 0 and cap % tg == 0
    n_exp = fill_lo.shape[0]
    return pl.pallas_call(
        functools.partial(_dispatch_kernel, tg=tg, n_exp=n_exp),
        grid_spec=pltpu.PrefetchScalarGridSpec(
            num_scalar_prefetch=2,
            grid=(n // tg,),
            in_specs=[pl.BlockSpec((tg * TOP_K,), lambda i, lo, hi: (i,), memory_space=pltpu.SMEM),
                      pl.BlockSpec((tg, d), lambda i, lo, hi: (i, 0))],
            out_specs=pl.BlockSpec(memory_space=pl.ANY),
            scratch_shapes=[pltpu.VMEM((2, tg, d), F32), pltpu.SemaphoreType.DMA((2,))]),
        out_shape=jax.ShapeDtypeStruct((cap, d), F32),
        compiler_params=_cparams("arbitrary"),
        name="moe_dispatch",
    )(fill_lo, fill_hi, dest_flat, h2)


def _gate_up_selector():
    sel = np.zeros((256, 256), np.float32)
    sel[2 * np.arange(128), np.arange(128)] = 1.0
    sel[2 * np.arange(128) + 1, 128 + np.arange(128)] = 1.0
    return jnp.asarray(sel, BF16)


def _expert_kernel(be_ref, nu_ref, x_ref, wgu_ref, bg_ref, bu_ref, wd_ref, bd_ref, sel_ref, y_ref,
                   wg_s, wu_s, wd_s):
    m = pl.program_id(0)
    active = m < nu_ref[0]
    new_expert = jnp.logical_or(m == 0, be_ref[m] != be_ref[jnp.maximum(m - 1, 0)])

    @pl.when(jnp.logical_and(active, new_expert))
    def _():
        for c in range(wgu_ref.shape[-1] // 256):
            t = jnp.dot(wgu_ref[0, :, c * 256:(c + 1) * 256].astype(BF16), sel_ref[...],
                        preferred_element_type=F32)
            wg_s[:, c * 128:(c + 1) * 128] = t[:, :128].astype(BF16)
            wu_s[:, c * 128:(c + 1) * 128] = t[:, 128:].astype(BF16)
        wd_s[...] = wd_ref[0].astype(BF16)

    @pl.when(active)
    def _():
        x = x_ref[...].astype(BF16)
        hg = jnp.dot(x, wg_s[...], preferred_element_type=F32) + bg_ref[0]
        hu = jnp.dot(x, wu_s[...], preferred_element_type=F32) + bu_ref[0]
        g = jnp.minimum(hg, SWIGLU_LIMIT)
        u = jnp.clip(hu, -SWIGLU_LIMIT, SWIGLU_LIMIT)
        a = (u + 1.0) * (g * jax.nn.sigmoid(SWIGLU_ALPHA * g))
        y_ref[...] = jnp.dot(a.astype(BF16), wd_s[...], preferred_element_type=F32) + bd_ref[0]

    @pl.when(jnp.logical_not(active))
    def _():
        y_ref[...] = jnp.zeros_like(y_ref)


def experts(xs, block_e, n_used, w_gu, bg, bu, w_down, bd):
    cap, d = xs.shape
    bm = BM
    _, _, f2 = w_gu.shape
    f = f2 // 2
    n_blocks = cap // bm

    def blk(m, be, nu):
        return jnp.minimum(m, nu[0] - 1)

    def exp_of(m, be, nu):
        return be[blk(m, be, nu)]

    per_expert = lambda shape: pl.BlockSpec(shape, lambda m, be, nu: (exp_of(m, be, nu), 0, 0))
    return pl.pallas_call(
        _expert_kernel,
        grid_spec=pltpu.PrefetchScalarGridSpec(
            num_scalar_prefetch=2,
            grid=(n_blocks,),
            in_specs=[pl.BlockSpec((bm, d), lambda m, be, nu: (blk(m, be, nu), 0)),
                      per_expert((1, d, f2)), per_expert((1, 1, f)), per_expert((1, 1, f)),
                      per_expert((1, f, d)), per_expert((1, 1, d)),
                      pl.BlockSpec((256, 256), lambda m, be, nu: (0, 0))],
            out_specs=pl.BlockSpec((bm, d), lambda m, be, nu: (m, 0)),
            scratch_shapes=[pltpu.VMEM((d, f), BF16), pltpu.VMEM((d, f), BF16), pltpu.VMEM((f, d), BF16)]),
        out_shape=jax.ShapeDtypeStruct((cap, d), F32),
        compiler_params=_cparams("arbitrary"),
        name="moe_experts",
    )(block_e, n_used, xs, w_gu, bg, bu, w_down, bd, _gate_up_selector())


def _combine_kernel(dest_ref, y_ref, x_ref, gt_ref, gate_ref, fg_ref, o_ref, buf, sems, *, tg, final_norm):
    s = pl.program_id(0)
    n_tiles = pl.num_programs(0) - 1
    cur = s % 2

    def row_copy(which, slot, k, t):
        return pltpu.make_async_copy(y_ref.at[pl.ds(slot, 1)], buf.at[which, k, pl.ds(t, 1)], sems.at[which])

    @pl.when(s < n_tiles)
    def _():
        def issue(g, c):
            for j in range(ROW_UNROLL):
                t = g * ROW_UNROLL + j
                for k in range(TOP_K):
                    row_copy(cur, dest_ref[t * TOP_K + k], k, t).start(priority=k % 2)
            return c

        lax.fori_loop(0, tg // ROW_UNROLL, issue, 0)

    @pl.when(s > 0)
    def _():
        prev = 1 - cur

        def drain(g, c):
            for _ in range(ROW_UNROLL * TOP_K):
                row_copy(prev, 0, 0, 0).wait()
            return c

        lax.fori_loop(0, tg // ROW_UNROLL, drain, 0)

        acc = gate_ref[:, 0:1] * buf[prev, 0]
        for k in range(1, TOP_K):
            acc += gate_ref[:, k:k + 1] * buf[prev, k]
        xn = x_ref[...] + gt_ref[0] * acc
        if final_norm:
            xn = _rms(xn) * fg_ref[...]
        o_ref[...] = xn


def combine(x_all, mods, y, dest_flat, gates, final_g, *, n_rows, seq, n_batch, final_norm):
    d = x_all.shape[1]
    tg = TG
    n_tiles = n_rows // tg

    def issued(s):
        return jnp.minimum(s, n_tiles - 1)

    def finished(s):
        return jnp.maximum(s - 1, 0)

    def mod_row(s):
        return jnp.minimum((finished(s) * tg) // seq, n_batch)

    return pl.pallas_call(
        functools.partial(_combine_kernel, tg=tg, final_norm=final_norm),
        grid=(n_tiles + 1,),
        in_specs=[pl.BlockSpec((tg * TOP_K,), lambda s: (issued(s),), memory_space=pltpu.SMEM),
                  pl.BlockSpec(memory_space=pl.ANY),
                  pl.BlockSpec((tg, d), lambda s: (finished(s), 0)),
                  pl.BlockSpec((1, 1, d), lambda s: (mod_row(s), 0, 5)),
                  pl.BlockSpec((tg, 128), lambda s: (finished(s), 0)),
                  pl.BlockSpec((1, d), lambda s: (0, 0))],
        out_specs=pl.BlockSpec((tg, d), lambda s: (finished(s), 0)),
        out_shape=jax.ShapeDtypeStruct((n_rows, d), F32),
        scratch_shapes=[pltpu.VMEM((2, TOP_K, tg, d), F32), pltpu.SemaphoreType.DMA((2,))],
        compiler_params=_cparams("arbitrary"),
        name="moe_combine",
    )(dest_flat, y, x_all, mods, gates, final_g)


def _rope_tables(seq, tm):
    half = HEAD_DIM // 2
    inv = ROPE_THETA ** (-jnp.arange(0, half, 2, dtype=F32) / half)
    t = jnp.arange(seq)
    lane = np.arange(128)
    dd = lane % HEAD_DIM
    use_col = (dd // half) == 1
    j = dd % (half // 2)
    second = (dd % half) >= (half // 2)
    pos = jnp.where(jnp.asarray(use_col)[None, :], (t % GRID_W)[:, None], (t // GRID_W)[:, None]).astype(F32)
    ang = pos * inv[jnp.asarray(j)][None, :]
    cos = jnp.cos(ang)
    sin = jnp.sin(ang) * jnp.where(jnp.asarray(second), 1.0, -1.0)[None, :]
    cos = jnp.concatenate([cos, jnp.ones((tm, 128), F32)], axis=0)
    sin = jnp.concatenate([sin, jnp.zeros((tm, 128), F32)], axis=0)
    return cos, sin


def _moe_plan(idx, rank, counts, n_exp, n_rows):
    bm = BM
    nk = n_rows * TOP_K
    n_blocks = -(-nk // bm) + n_exp
    counts = counts.reshape(n_exp)
    padded = (counts + bm - 1) // bm * bm
    pend = jnp.cumsum(padded)
    pstart = pend - padded
    dest = (pstart[idx] + rank).reshape(nk).astype(jnp.int32)
    block_start = jnp.arange(n_blocks, dtype=jnp.int32) * bm
    block_e = jnp.minimum(jnp.sum(pend[None, :] <= block_start[:, None], axis=1), n_exp - 1).astype(jnp.int32)
    n_used = (pend[-1:] // bm).astype(jnp.int32)
    return dest, block_e, n_used, (pstart + counts).astype(jnp.int32), pend.astype(jnp.int32), n_blocks * bm


def kernel(x, c, ctx, c_ctx, ada_w, ada_b, norm1_g, norm2_g, w_in, conv_w, na_rpb, q_norm_g, k_norm_g,
           w_br_conv, w_br_na, w_br_gqa, w_out, router_w, router_b, w_gu, b_gu, w_down, b_down, final_g):
    n_batch, seq, d = x.shape
    n_ctx = ctx.shape[1]
    depth = ada_w.shape[0]
    n_exp = router_w.shape[-1]
    n_lat = n_batch * seq
    n_all = n_lat + n_batch * n_ctx
    assert n_batch + 1 <= MOD_ROWS and seq % TM == 0 and (n_batch * n_ctx) % TM == 0

    cond = jnp.zeros((MOD_ROWS, d), F32).at[:n_batch].set(c).at[n_batch].set(c_ctx)
    mods_all = ada_mods(cond, ada_w, ada_b)
    cos_t, sin_t = _rope_tables(seq, TM)
    hsum = jnp.asarray(np.kron(np.eye(256 // HEAD_DIM), np.ones((HEAD_DIM, HEAD_DIM))), BF16)

    f = w_gu.shape[-1] // 2
    w_gu_flat = w_gu.reshape(depth * n_exp, d, 2 * f)
    bg = b_gu[..., 0::2].reshape(depth * n_exp, 1, f)
    bu = b_gu[..., 1::2].reshape(depth * n_exp, 1, f)
    w_down_flat = w_down.reshape(depth * n_exp, f, d)
    bd = b_down.reshape(depth * n_exp, 1, d)

    x_parts = (x.reshape(n_lat, d), ctx.reshape(n_batch * n_ctx, d))
    for l in range(depth):
        last = l == depth - 1
        mods = mods_all[l].reshape(MOD_ROWS, 1, 6 * d)
        qg = jnp.tile(q_norm_g[l], 256 // HEAD_DIM).reshape(1, 256)
        kg = jnp.tile(k_norm_g[l], 128 // HEAD_DIM).reshape(1, 128)
        conv, naq, nak, nav, gq, gk, gv, gates = in_projection(
            x_parts, mods, norm1_g[l].reshape(1, d), w_in[l].astype(BF16), cos_t, sin_t, qg, kg, hsum,
            seq=seq, n_lat=n_lat, n_batch=n_batch)

        y_na = neighborhood_attention(naq, nak, nav, na_bias_table(na_rpb[l] * LOG2E, seq),
                                      seq=seq, ctx=n_ctx, n_batch=n_batch)

        def kv_all(t):
            return jnp.concatenate([t[:n_lat].reshape(n_batch, seq, GQA_KV_WIDTH),
                                    t[n_lat:].reshape(n_batch, n_ctx, GQA_KV_WIDTH)], axis=1)

        s_all = seq + n_ctx
        kt_all = (kv_all(gk).transpose(0, 2, 1) * GQA_K_FP8_SCALE).astype(FP8)
        kt_all = kt_all.reshape(n_batch, GQA_KV_HEADS, HEAD_DIM, s_all)
        gv_all = kv_all(gv)
        v_tail = jnp.concatenate([jnp.ones((n_batch, s_all, 1), BF16),
                                  jnp.zeros((n_batch, s_all, HEAD_DIM - 1), BF16)], axis=-1)
        v_all = jnp.stack([jnp.concatenate([gv_all[..., kv * HEAD_DIM:(kv + 1) * HEAD_DIM], v_tail], axis=-1)
                           for kv in range(GQA_KV_HEADS)], axis=1)
        y_gqa = gqa_attention(gq, kt_all, v_all, seq=seq, n_batch=n_batch)

        n_rows = n_lat if last else n_all
        y_na_parts, y_gqa_parts = (y_na,), (y_gqa,)
        if not last:
            yc_na, yc_gqa = ctx_attention(naq, nak, nav, gq, gk, gv, ctx=n_ctx, n_lat=n_lat, n_batch=n_batch)
            y_na_parts, y_gqa_parts = (y_na, yc_na), (y_gqa, yc_gqa)
        x_all = merge_branches(x_parts, mods, conv, y_na_parts, y_gqa_parts, gates, conv_w[l],
                               w_br_conv[l].astype(BF16), w_br_na[l].astype(BF16), w_br_gqa[l].astype(BF16),
                               w_out[l].astype(BF16), n_rows=n_rows, seq=seq, ctx=n_ctx, n_lat=n_lat,
                               n_batch=n_batch)

        h2, idx, gate_w, rank, counts = router(x_all, mods, norm2_g[l].reshape(1, d), router_w[l], router_b[l],
                                               n_rows=n_rows, seq=seq, n_batch=n_batch)
        dest, block_e, n_used, fill_lo, fill_hi, cap = _moe_plan(
            idx[:, :TOP_K], rank[:, :TOP_K], counts, n_exp, n_rows)
        xs = dispatch(h2, dest, fill_lo, fill_hi, cap)
        y = experts(xs, block_e + l * n_exp, n_used, w_gu_flat, bg, bu, w_down_flat, bd)
        x_all = combine(x_all, mods, y, dest, gate_w, final_g.reshape(1, d),
                        n_rows=n_rows, seq=seq, n_batch=n_batch, final_norm=last)
        x_parts = (x_all,)
    return x_all.reshape(n_batch, seq, d)
```

```python
import functools

import jax
import jax.numpy as jnp
import numpy as np
from jax import lax
from jax.experimental import pallas as pl
from jax.experimental.pallas import tpu as pltpu

F32 = jnp.float32
BF16 = jnp.bfloat16

GRID_W = 64
EPS = 1e-6
HEAD_DIM = 64
CONV_WIDTH = 512
NA_HEADS = 8
NA_WIDTH = NA_HEADS * HEAD_DIM
NA_KH = 8
NA_KW = 16
NA_QROWS = 4
NA_BAND = NA_QROWS + NA_KH
GQA_Q_HEADS = 8
GQA_KV_HEADS = 2
GQA_GROUP = GQA_Q_HEADS // GQA_KV_HEADS
GQA_WIDTH = GQA_Q_HEADS * HEAD_DIM
GQA_KV_WIDTH = GQA_KV_HEADS * HEAD_DIM
ROPE_THETA = 10000.0
TOP_K = 4
SWIGLU_ALPHA = 1.702
SWIGLU_LIMIT = 7.0
LOG2E = float(np.log2(np.e))
ATTN_Q_SCALE = HEAD_DIM ** -0.5 * LOG2E
GQA_Q_SCALE = ATTN_Q_SCALE
FP8 = jnp.float8_e4m3fn
GQA_K_FP8_SCALE = 0.125
MOD_ROWS = 16
NEG_BIG = -1e30
VMEM_LIMIT = 56 * 1024 * 1024

TM = 512
TQ = 512
GQA_KEY_CHUNK = 2048
BM = 512
TG = 256
ROW_UNROLL = 8


def _cparams(*sem):
    return pltpu.CompilerParams(dimension_semantics=sem, vmem_limit_bytes=VMEM_LIMIT)


def _rms(x):
    return x * lax.rsqrt(jnp.mean(x * x, axis=-1, keepdims=True) + EPS)


def _dot_bf16x3(a, b):
    a_hi = a.astype(BF16)
    b_hi = b.astype(BF16)
    a_lo = (a - a_hi.astype(F32)).astype(BF16)
    b_lo = (b - b_hi.astype(F32)).astype(BF16)
    dot = functools.partial(jnp.dot, preferred_element_type=F32)
    return dot(a_hi, b_hi) + (dot(a_lo, b_hi) + dot(a_hi, b_lo))


def _two_part_specs(tm, width, n_first):
    return [pl.BlockSpec((tm, width), lambda i: (jnp.minimum(i, n_first - 1), 0)),
            pl.BlockSpec((tm, width), lambda i: (jnp.maximum(i - n_first, 0), 0))]


def _two_part_read(a_ref, b_ref, n_first):
    return jnp.where(pl.program_id(0) < n_first, a_ref[...], b_ref[...])


def _two_parts(parts, tm):
    if len(parts) == 1:
        return parts[0], parts[0], parts[0].shape[0] // tm, parts[0].shape[0]
    return parts[0], parts[1], parts[0].shape[0] // tm, parts[0].shape[0] + parts[1].shape[0]


def _ada_kernel(c_ref, w_ref, b_ref, o_ref):
    c = c_ref[...]
    s = c * jax.nn.sigmoid(c)
    o_ref[0] = jnp.dot(s, w_ref[0], precision=lax.Precision.HIGHEST,
                       preferred_element_type=F32) + b_ref[0]


def ada_mods(cond, ada_w, ada_b):
    depth, d, n6 = ada_w.shape
    tn = d
    return pl.pallas_call(
        _ada_kernel,
        grid=(depth, n6 // tn),
        in_specs=[pl.BlockSpec((MOD_ROWS, d), lambda l, j: (0, 0)),
                  pl.BlockSpec((1, d, tn), lambda l, j: (l, 0, j)),
                  pl.BlockSpec((1, 1, tn), lambda l, j: (l, 0, j))],
        out_specs=pl.BlockSpec((1, MOD_ROWS, tn), lambda l, j: (l, 0, j)),
        out_shape=jax.ShapeDtypeStruct((depth, MOD_ROWS, n6), F32),
        compiler_params=_cparams("arbitrary", "arbitrary"),
        name="ada_mods",
    )(cond, ada_w, ada_b.reshape(depth, 1, n6))


def _head_inv_rms(t, hsum_ref):
    tot = jnp.dot((t * t).astype(BF16), hsum_ref[...], preferred_element_type=F32)
    return lax.rsqrt(tot * (1.0 / HEAD_DIM) + EPS)


def _rope(t, cos, sin):
    lane = lax.broadcasted_iota(jnp.int32, t.shape, 1)
    second = (lane % 32) >= 16
    partner = jnp.where(second, pltpu.roll(t, 16, 1), pltpu.roll(t, 128 - 16, 1))
    return t * cos + partner * sin


def _inproj_kernel(xa_ref, xb_ref, sh_ref, sc_ref, g_ref, w_ref, cos_ref, sin_ref, qg_ref, kg_ref, hsum_ref,
                   conv_o, naq_o, nak_o, nav_o, gq_o, gk_o, gv_o, gate_o, h_scr, *, d, n_first):
    h = _rms(_two_part_read(xa_ref, xb_ref, n_first)) * g_ref[...]
    h = h * (1.0 + sc_ref[0]) + sh_ref[0]
    h_scr[...] = h.astype(BF16)

    def proj(a, n):
        return jnp.dot(h_scr[...], w_ref[:, a:a + n], preferred_element_type=F32)

    off = 0
    for j in range(3):
        conv_o[:, j * CONV_WIDTH:(j + 1) * CONV_WIDTH] = proj(off, CONV_WIDTH).astype(BF16)
        off += CONV_WIDTH
    naq_o[...] = (proj(off, NA_WIDTH) * ATTN_Q_SCALE).astype(BF16)
    off += NA_WIDTH
    nak_o[...] = proj(off, NA_WIDTH).astype(BF16)
    off += NA_WIDTH
    nav_o[...] = proj(off, NA_WIDTH).astype(BF16)
    off += NA_WIDTH
    cos = cos_ref[...]
    sin = sin_ref[...]
    for j in range(GQA_WIDTH // 256):
        t = proj(off + j * 256, 256)
        t = t * _head_inv_rms(t, hsum_ref) * qg_ref[...]
        for half in range(2):
            lanes = slice(half * 128, (half + 1) * 128)
            gq_o[:, j * 256 + half * 128:j * 256 + (half + 1) * 128] = (
                _rope(t[:, lanes], cos, sin) * GQA_Q_SCALE).astype(BF16)
    off += GQA_WIDTH
    t = proj(off, 2 * GQA_KV_WIDTH)
    inv = _head_inv_rms(t, hsum_ref)
    gk_o[...] = _rope(t[:, :GQA_KV_WIDTH] * inv[:, :GQA_KV_WIDTH] * kg_ref[...], cos, sin).astype(BF16)
    gv_o[...] = t[:, GQA_KV_WIDTH:].astype(BF16)
    off += 2 * GQA_KV_WIDTH
    for j in range(3):
        gate_o[:, j * d:(j + 1) * d] = jax.nn.sigmoid(proj(off + j * d, d)).astype(BF16)


def in_projection(x_parts, mods, g1, w_in_bf, cos_t, sin_t, qg, kg, hsum, *, seq, n_lat, n_batch):
    tm = TM
    x_a, x_b, n_first, nt = _two_parts(x_parts, tm)
    d = x_a.shape[1]
    n_lat_tiles = n_lat // tm
    seq_tiles = seq // tm
    in_cols = w_in_bf.shape[1]

    def mod_row(i):
        return jnp.minimum((i * tm) // seq, n_batch)

    def rope_blk(i):
        return jnp.where(i < n_lat_tiles, i % seq_tiles, seq_tiles)

    tok = lambda w: pl.BlockSpec((tm, w), lambda i: (i, 0))
    const = lambda shape: pl.BlockSpec(shape, lambda i: tuple(0 for _ in shape))
    outs = [CONV_WIDTH * 3, NA_WIDTH, NA_WIDTH, NA_WIDTH, GQA_WIDTH, GQA_KV_WIDTH, GQA_KV_WIDTH, 3 * d]
    return pl.pallas_call(
        functools.partial(_inproj_kernel, d=d, n_first=n_first),
        grid=(nt // tm,),
        in_specs=_two_part_specs(tm, d, n_first) + [
                  pl.BlockSpec((1, 1, d), lambda i: (mod_row(i), 0, 0)),
                  pl.BlockSpec((1, 1, d), lambda i: (mod_row(i), 0, 1)),
                  const((1, d)),
                  pl.BlockSpec((d, in_cols), lambda i: (0, 0), pipeline_mode=pl.Buffered(1)),
                  pl.BlockSpec((tm, 128), lambda i: (rope_blk(i), 0)),
                  pl.BlockSpec((tm, 128), lambda i: (rope_blk(i), 0)),
                  const((1, 256)), const((1, 128)), const((256, 256))],
        out_specs=[tok(w) for w in outs],
        out_shape=[jax.ShapeDtypeStruct((nt, w), BF16) for w in outs],
        scratch_shapes=[pltpu.VMEM((tm, d), BF16)],
        compiler_params=_cparams("arbitrary"),
        name="in_projection",
    )(x_a, x_b, mods, mods, g1, w_in_bf, cos_t, sin_t, qg, kg, hsum)


def _softmax_pv(scores, values, exp=jnp.exp):
    m = functools.reduce(jnp.maximum, [jnp.max(s, axis=-1, keepdims=True) for s in scores])
    ps = [exp(s - m) for s in scores]
    l = functools.reduce(lambda a, b: a + b, [jnp.sum(p, axis=-1, keepdims=True) for p in ps])
    o = functools.reduce(lambda a, b: a + b,
                         [jnp.dot(p.astype(BF16), v, preferred_element_type=F32) for p, v in zip(ps, values)])
    return o / l


def _qk(q, k):
    return lax.dot_general(q, k, (((1,), (1,)), ((), ())), preferred_element_type=F32)


def _na_kernel(q_ref, k_ref, v_ref, kc_ref, vc_ref, bias_ref, o_ref, *, w, rows):
    j = pl.program_id(1)
    start = jnp.clip(j * NA_QROWS - NA_KH // 2, 0, rows - NA_BAND) * w
    start = pl.multiple_of(start, w)
    nkeys = NA_BAND * w
    first_head = lax.broadcasted_iota(jnp.int32, (1, 2 * HEAD_DIM), 1) < HEAD_DIM
    for pair in range(NA_HEADS // 2):
        ls = slice(pair * 2 * HEAD_DIM, (pair + 1) * 2 * HEAD_DIM)
        q2 = q_ref[:, ls]
        kw = k_ref[pl.ds(start, nkeys), ls]
        vw = v_ref[pl.ds(start, nkeys), ls]
        kc = kc_ref[:, ls]
        vc = vc_ref[:, ls]
        outs = []
        for e in range(2):
            q = jnp.where(first_head == (e == 0), q2, jnp.zeros_like(q2))
            s_win = _qk(q, kw) + bias_ref[0, 2 * pair + e]
            s_ctx = _qk(q, kc)
            outs.append(_softmax_pv([s_win, s_ctx], [vw, vc], exp=jnp.exp2))
        o_ref[:, ls] = jnp.where(first_head, outs[0], outs[1]).astype(BF16)


def neighborhood_attention(naq, nak, nav, bias, *, seq, ctx, n_batch):
    w = GRID_W
    rows = seq // w
    nj = rows // NA_QROWS
    nq = NA_QROWS * w
    n_lat = n_batch * seq
    ctx_blk0 = n_lat // ctx

    def case(j):
        return jnp.where(j == 0, 0, jnp.where(j == nj - 1, 2, 1))

    return pl.pallas_call(
        functools.partial(_na_kernel, w=w, rows=rows),
        grid=(n_batch, nj),
        in_specs=[pl.BlockSpec((nq, NA_WIDTH), lambda b, j: (b * nj + j, 0)),
                  pl.BlockSpec((seq, NA_WIDTH), lambda b, j: (b, 0)),
                  pl.BlockSpec((seq, NA_WIDTH), lambda b, j: (b, 0)),
                  pl.BlockSpec((ctx, NA_WIDTH), lambda b, j: (ctx_blk0 + b, 0)),
                  pl.BlockSpec((ctx, NA_WIDTH), lambda b, j: (ctx_blk0 + b, 0)),
                  pl.BlockSpec((1, NA_HEADS, nq, NA_BAND * w), lambda b, j: (case(j), 0, 0, 0))],
        out_specs=pl.BlockSpec((nq, NA_WIDTH), lambda b, j: (b * nj + j, 0)),
        out_shape=jax.ShapeDtypeStruct((n_lat, NA_WIDTH), BF16),
        compiler_params=_cparams("arbitrary", "arbitrary"),
        name="neighborhood_attention",
    )(naq, nak, nav, nak, nav, bias)


def na_bias_table(rpb, seq):
    w = GRID_W
    rows = seq // w
    nj = rows // NA_QROWS
    n_heads = rpb.shape[0]
    cols = np.arange(w)
    col_start = np.clip(cols - NA_KW // 2, 0, w - NA_KW)
    col_ok = (cols[None, :] >= col_start[:, None]) & (cols[None, :] < col_start[:, None] + NA_KW)
    pad = max(0, w - NA_KW)
    rp = jnp.pad(rpb.astype(F32), ((0, 0), (0, 0), (pad, pad)))
    tcol = jnp.stack([rp[:, :, pad + NA_KW - 1 - c: pad + NA_KW - 1 - c + w] for c in range(w)], axis=2)
    tcol = jnp.where(jnp.asarray(col_ok)[None, None], tcol, NEG_BIG)
    masked = jnp.full((n_heads, w, w), NEG_BIG, F32)
    tabs = []
    for j in (0, 1, nj - 1):
        start = int(np.clip(j * NA_QROWS - NA_KH // 2, 0, rows - NA_BAND))
        q_rows = []
        for a in range(NA_QROWS):
            r = j * NA_QROWS + a
            rs = int(np.clip(r - NA_KH // 2, 0, rows - NA_KH))
            blocks = []
            for i in range(NA_BAND):
                kr = start + i
                blocks.append(tcol[:, kr - r + NA_KH - 1] if rs <= kr < rs + NA_KH else masked)
            q_rows.append(jnp.concatenate(blocks, axis=-1))
        tabs.append(jnp.concatenate(q_rows, axis=1))
    return jnp.stack(tabs)


def _online_softmax_pv(q, kt_ref, v_ref, chunks):
    m = jnp.full((q.shape[0], 1), -jnp.inf, F32)
    acc = jnp.zeros((q.shape[0], v_ref.shape[-1]), F32)
    for a, n in chunks:
        s = jnp.dot(q, kt_ref[:, a:a + n], preferred_element_type=F32)
        m_new = jnp.maximum(m, jnp.max(s, axis=-1, keepdims=True))
        p = jnp.exp2(s - m_new).astype(BF16)
        acc = acc * jnp.exp2(m - m_new) + jnp.dot(p, v_ref[a:a + n, :], preferred_element_type=F32)
        m = m_new
    return acc[:, :HEAD_DIM] / acc[:, HEAD_DIM:HEAD_DIM + 1]


def _gqa_kernel(q_ref, kt_ref, v_ref, o_ref, *, tq, chunks):
    q = jnp.concatenate([q_ref[:, g * HEAD_DIM:(g + 1) * HEAD_DIM] for g in range(GQA_GROUP)], axis=0)
    q = (q * (1.0 / GQA_K_FP8_SCALE)).astype(FP8)
    o = _online_softmax_pv(q, kt_ref.at[0, 0], v_ref.at[0, 0], chunks)
    for g in range(GQA_GROUP):
        o_ref[:, g * HEAD_DIM:(g + 1) * HEAD_DIM] = o[g * tq:(g + 1) * tq].astype(BF16)


def _key_chunks(total, size):
    return tuple((a, min(size, total - a)) for a in range(0, total, size))


def gqa_attention(gq, kt_all, v_all, *, seq, n_batch):
    tq = TQ
    nq = seq // tq
    s_all = kt_all.shape[-1]
    vw = v_all.shape[-1]
    gw = GQA_GROUP * HEAD_DIM
    return pl.pallas_call(
        functools.partial(_gqa_kernel, tq=tq, chunks=_key_chunks(s_all, GQA_KEY_CHUNK)),
        grid=(n_batch, GQA_KV_HEADS, nq),
        in_specs=[pl.BlockSpec((tq, gw), lambda b, kv, i: (b * nq + i, kv)),
                  pl.BlockSpec((1, 1, HEAD_DIM, s_all), lambda b, kv, i: (b, kv, 0, 0)),
                  pl.BlockSpec((1, 1, s_all, vw), lambda b, kv, i: (b, kv, 0, 0))],
        out_specs=pl.BlockSpec((tq, gw), lambda b, kv, i: (b * nq + i, kv)),
        out_shape=jax.ShapeDtypeStruct((n_batch * seq, GQA_WIDTH), BF16),
        compiler_params=_cparams("arbitrary", "arbitrary", "arbitrary"),
        name="gqa_attention",
    )(gq, kt_all, v_all)


def _ctx_attn_kernel(naq_ref, nak_ref, nav_ref, gq_ref, gk_ref, gv_ref, ona_ref, ogqa_ref):
    for h in range(NA_HEADS):
        hs = slice(h * HEAD_DIM, (h + 1) * HEAD_DIM)
        ona_ref[:, hs] = _softmax_pv([_qk(naq_ref[:, hs], nak_ref[:, hs])], [nav_ref[:, hs]],
                                     exp=jnp.exp2).astype(BF16)
    for h in range(GQA_Q_HEADS):
        hs = slice(h * HEAD_DIM, (h + 1) * HEAD_DIM)
        kv = h // GQA_GROUP
        ks = slice(kv * HEAD_DIM, (kv + 1) * HEAD_DIM)
        ogqa_ref[:, hs] = _softmax_pv([_qk(gq_ref[:, hs], gk_ref[:, ks])], [gv_ref[:, ks]],
                                      exp=jnp.exp2).astype(BF16)


def ctx_attention(naq, nak, nav, gq, gk, gv, *, ctx, n_lat, n_batch):
    blk0 = n_lat // ctx
    wide = lambda: pl.BlockSpec((ctx, NA_WIDTH), lambda b: (blk0 + b, 0))
    narrow = lambda: pl.BlockSpec((ctx, GQA_KV_WIDTH), lambda b: (blk0 + b, 0))
    out = lambda: pl.BlockSpec((ctx, NA_WIDTH), lambda b: (b, 0))
    return pl.pallas_call(
        _ctx_attn_kernel,
        grid=(n_batch,),
        in_specs=[wide(), wide(), wide(), wide(), narrow(), narrow()],
        out_specs=[out(), out()],
        out_shape=[jax.ShapeDtypeStruct((n_batch * ctx, NA_WIDTH), BF16)] * 2,
        compiler_params=_cparams("arbitrary"),
        name="ctx_attention",
    )(naq, nak, nav, gq, gk, gv)


def _merge_kernel(xa_ref, xb_ref, gt_ref, conv_ref, prev_ref, next_ref, yna_a, yna_b, ygqa_a, ygqa_b, gate_ref,
                  cw_ref, wc_ref, wn_ref, wg_ref, wo_ref, o_ref, *, tm, d, seq, ctx, n_lat, n_first_x, n_first_y):
    i = pl.program_id(0)
    cw = CONV_WIDTH

    def u_of(ref, rows):
        return ref[rows, cw:2 * cw].astype(F32) * ref[rows, 2 * cw:3 * cw].astype(F32)

    u = u_of(conv_ref, slice(None))
    hp = u_of(prev_ref, slice(15, 16))
    hn = u_of(next_ref, slice(0, 1))
    row = lax.broadcasted_iota(jnp.int32, (tm, 1), 0)
    r = i * tm + row
    pos = jnp.where(r < n_lat, r % seq, r % ctx)
    length = jnp.where(r < n_lat, seq, ctx)
    u_prev = jnp.where(row == 0, hp, pltpu.roll(u, 1, 0))
    u_prev = jnp.where(pos == 0, 0.0, u_prev)
    u_next = jnp.where(row == tm - 1, hn, pltpu.roll(u, tm - 1, 0))
    u_next = jnp.where(pos == length - 1, 0.0, u_next)
    y_conv = conv_ref[:, 0:cw].astype(F32) * (u_prev * cw_ref[0:1] + u * cw_ref[1:2] + u_next * cw_ref[2:3])

    m = gate_ref[:, 0:d].astype(F32) * jnp.dot(y_conv.astype(BF16), wc_ref[...], preferred_element_type=F32)
    y_na = _two_part_read(yna_a, yna_b, n_first_y)
    y_gqa = _two_part_read(ygqa_a, ygqa_b, n_first_y)
    m += gate_ref[:, d:2 * d].astype(F32) * jnp.dot(y_na, wn_ref[...], preferred_element_type=F32)
    m += gate_ref[:, 2 * d:3 * d].astype(F32) * jnp.dot(y_gqa, wg_ref[...], preferred_element_type=F32)
    out = jnp.dot(m.astype(BF16), wo_ref[...], preferred_element_type=F32)
    o_ref[...] = _two_part_read(xa_ref, xb_ref, n_first_x) + gt_ref[0] * out


def merge_branches(x_parts, mods, conv, y_na_parts, y_gqa_parts, gates, conv_w, wc, wn, wg, wo, *, n_rows, seq,
                   ctx, n_lat, n_batch):
    tm = TM
    x_a, x_b, n_first_x, _ = _two_parts(x_parts, tm)
    yna_a, yna_b, n_first_y, _ = _two_parts(y_na_parts, tm)
    ygqa_a, ygqa_b, _, _ = _two_parts(y_gqa_parts, tm)
    d = x_a.shape[1]
    halo = 16
    n_halo_blocks = conv.shape[0] // halo
    per = tm // halo

    def mod_row(i):
        return jnp.minimum((i * tm) // seq, n_batch)

    tok = lambda w: pl.BlockSpec((tm, w), lambda i: (i, 0))
    const = lambda shape: pl.BlockSpec(shape, lambda i: tuple(0 for _ in shape))
    return pl.pallas_call(
        functools.partial(_merge_kernel, tm=tm, d=d, seq=seq, ctx=ctx, n_lat=n_lat,
                          n_first_x=n_first_x, n_first_y=n_first_y),
        grid=(n_rows // tm,),
        in_specs=_two_part_specs(tm, d, n_first_x) + [
                  pl.BlockSpec((1, 1, d), lambda i: (mod_row(i), 0, 2)),
                  tok(3 * CONV_WIDTH),
                  pl.BlockSpec((halo, 3 * CONV_WIDTH), lambda i: (jnp.maximum(i * per - 1, 0), 0)),
                  pl.BlockSpec((halo, 3 * CONV_WIDTH),
                               lambda i: (jnp.minimum((i + 1) * per, n_halo_blocks - 1), 0)),
                  *_two_part_specs(tm, NA_WIDTH, n_first_y), *_two_part_specs(tm, GQA_WIDTH, n_first_y),
                  tok(3 * d),
                  const((3, CONV_WIDTH)), const((CONV_WIDTH, d)), const((NA_WIDTH, d)),
                  const((GQA_WIDTH, d)), const((d, d))],
        out_specs=tok(d),
        out_shape=jax.ShapeDtypeStruct((n_rows, d), F32),
        compiler_params=_cparams("arbitrary"),
        name="merge_branches",
    )(x_a, x_b, mods, conv, conv, conv, yna_a, yna_b, ygqa_a, ygqa_b, gates, conv_w, wc, wn, wg, wo)


def _router_kernel(x_ref, sh_ref, sc_ref, g_ref, rw_ref, rb_ref, tri_ref,
                   h_o, idx_o, gate_o, rank_o, cnt_o, carry, *, n_exp):
    i = pl.program_id(0)

    @pl.when(i == 0)
    def _():
        carry[...] = jnp.zeros_like(carry)

    h = _rms(x_ref[...]) * g_ref[...]
    h = h * (1.0 + sc_ref[0]) + sh_ref[0]
    h_o[...] = h
    logits = _dot_bf16x3(h, rw_ref[...]) + rb_ref[...]
    lane = lax.broadcasted_iota(jnp.int32, logits.shape, 1)
    work = logits
    vals, idxs, hots = [], [], []
    for _ in range(TOP_K):
        m = jnp.max(work, axis=-1, keepdims=True)
        ik = jnp.min(jnp.where(work == m, lane, n_exp), axis=-1, keepdims=True)
        hot = lane == ik
        vals.append(m)
        idxs.append(ik)
        hots.append(hot)
        work = jnp.where(hot, -jnp.inf, work)
    es = [jnp.exp(v - vals[0]) for v in vals]
    den = functools.reduce(lambda a, b: a + b, es)
    hot_sum = functools.reduce(lambda a, b: a + b, [jnp.where(hh, 1.0, 0.0) for hh in hots])
    prefix = jnp.dot(tri_ref[...], hot_sum.astype(BF16), preferred_element_type=F32) + carry[...]
    ranks = [jnp.sum(jnp.where(hh, prefix, 0.0), axis=-1, keepdims=True) for hh in hots]
    carry[...] = carry[...] + jnp.sum(hot_sum, axis=0, keepdims=True)
    cnt_o[...] = carry[...].astype(jnp.int32)

    out_lane = lax.broadcasted_iota(jnp.int32, idx_o.shape, 1)

    def spread(cols, fill):
        acc = jnp.full(idx_o.shape, fill, cols[0].dtype)
        for k, c in enumerate(cols):
            acc = jnp.where(out_lane == k, c, acc)
        return acc

    idx_o[...] = spread(idxs, 0)
    gate_o[...] = spread([e / den for e in es], 0.0)
    rank_o[...] = spread([rk.astype(jnp.int32) for rk in ranks], 0)


def router(x_all, mods, g2, router_w, router_b, *, n_rows, seq, n_batch):
    d = x_all.shape[1]
    n_exp = router_w.shape[1]
    tm = TM
    tri = jnp.asarray(np.tril(np.ones((tm, tm), np.float32), -1), BF16)

    def mod_row(i):
        return jnp.minimum((i * tm) // seq, n_batch)

    tok = lambda w: pl.BlockSpec((tm, w), lambda i: (i, 0))
    const = lambda shape: pl.BlockSpec(shape, lambda i: tuple(0 for _ in shape))
    return pl.pallas_call(
        functools.partial(_router_kernel, n_exp=n_exp),
        grid=(n_rows // tm,),
        in_specs=[tok(d),
                  pl.BlockSpec((1, 1, d), lambda i: (mod_row(i), 0, 3)),
                  pl.BlockSpec((1, 1, d), lambda i: (mod_row(i), 0, 4)),
                  const((1, d)), const((d, n_exp)), const((1, n_exp)), const((tm, tm))],
        out_specs=[tok(d), tok(128), tok(128), tok(128), const((1, n_exp))],
        out_shape=[jax.ShapeDtypeStruct((n_rows, d), F32),
                   jax.ShapeDtypeStruct((n_rows, 128), jnp.int32),
                   jax.ShapeDtypeStruct((n_rows, 128), F32),
                   jax.ShapeDtypeStruct((n_rows, 128), jnp.int32),
                   jax.ShapeDtypeStruct((1, n_exp), jnp.int32)],
        scratch_shapes=[pltpu.VMEM((1, n_exp), F32)],
        compiler_params=_cparams("arbitrary"),
        name="router",
    )(x_all, mods, mods, g2, router_w, router_b.reshape(1, n_exp), tri)


def _dispatch_kernel(fill_lo_ref, fill_hi_ref, dest_ref, h_ref, xs_ref, hbuf, sems, *, tg, n_exp):
    i = pl.program_id(0)
    last = pl.num_programs(0) - 1
    cur = i % 2

    hbuf[cur] = h_ref[...]

    def row_copy(which, t, slot):
        return pltpu.make_async_copy(hbuf.at[which, pl.ds(t, 1)], xs_ref.at[pl.ds(slot, 1)], sems.at[which])

    def issue(g, c):
        for j in range(ROW_UNROLL):
            t = g * ROW_UNROLL + j
            for k in range(TOP_K):
                row_copy(cur, t, dest_ref[t * TOP_K + k]).start(priority=k % 2)
        return c

    lax.fori_loop(0, tg // ROW_UNROLL, issue, 0)

    def drain(which):
        def body(g, c):
            for _ in range(ROW_UNROLL * TOP_K):
                row_copy(which, 0, 0).wait()
            return c

        lax.fori_loop(0, tg // ROW_UNROLL, body, 0)

    @pl.when(i > 0)
    def _():
        drain(1 - cur)

    @pl.when(i == last)
    def _():
        drain(cur)
        for e in range(n_exp):
            lo = fill_lo_ref[e]
            hi = fill_hi_ref[e]

            def fill(p, c):
                row_copy(cur, 0, p).start()
                return c

            lax.fori_loop(lo, hi, fill, 0)

            def fill_wait(p, c):
                row_copy(cur, 0, 0).wait()
                return c

            lax.fori_loop(lo, hi, fill_wait, 0)

        def tail_copy(blk):
            return pltpu.make_async_copy(hbuf.at[cur], xs_ref.at[pl.ds(pl.multiple_of(blk * tg, tg), tg)],
                                         sems.at[cur])

        first_free = fill_hi_ref[n_exp - 1] // tg

        def tail(blk, c):
            tail_copy(blk).start()
            tail_copy(blk).wait()
            return c

        lax.fori_loop(first_free, xs_ref.shape[0] // tg, tail, 0)


def dispatch(h2, dest_flat, fill_lo, fill_hi, cap):
    n, d = h2.shape
    tg = TG
    assert BM % tg == 0 and cap % tg == 0
    n_exp = fill_lo.shape[0]
    return pl.pallas_call(
        functools.partial(_dispatch_kernel, tg=tg, n_exp=n_exp),
        grid_spec=pltpu.PrefetchScalarGridSpec(
            num_scalar_prefetch=2,
            grid=(n // tg,),
            in_specs=[pl.BlockSpec((tg * TOP_K,), lambda i, lo, hi: (i,), memory_space=pltpu.SMEM),
                      pl.BlockSpec((tg, d), lambda i, lo, hi: (i, 0))],
            out_specs=pl.BlockSpec(memory_space=pl.ANY),
            scratch_shapes=[pltpu.VMEM((2, tg, d), F32), pltpu.SemaphoreType.DMA((2,))]),
        out_shape=jax.ShapeDtypeStruct((cap, d), F32),
        compiler_params=_cparams("arbitrary"),
        name="moe_dispatch",
    )(fill_lo, fill_hi, dest_flat, h2)


def _gate_up_selector():
    sel = np.zeros((256, 256), np.float32)
    sel[2 * np.arange(128), np.arange(128)] = 1.0
    sel[2 * np.arange(128) + 1, 128 + np.arange(128)] = 1.0
    return jnp.asarray(sel, BF16)


def _expert_kernel(be_ref, nu_ref, x_ref, wgu_ref, bg_ref, bu_ref, wd_ref, bd_ref, sel_ref, y_ref,
                   wg_s, wu_s, wd_s):
    m = pl.program_id(0)
    active = m < nu_ref[0]
    new_expert = jnp.logical_or(m == 0, be_ref[m] != be_ref[jnp.maximum(m - 1, 0)])

    @pl.when(jnp.logical_and(active, new_expert))
    def _():
        for c in range(wgu_ref.shape[-1] // 256):
            t = jnp.dot(wgu_ref[0, :, c * 256:(c + 1) * 256].astype(BF16), sel_ref[...],
                        preferred_element_type=F32)
            wg_s[:, c * 128:(c + 1) * 128] = t[:, :128].astype(BF16)
            wu_s[:, c * 128:(c + 1) * 128] = t[:, 128:].astype(BF16)
        wd_s[...] = wd_ref[0].astype(BF16)

    @pl.when(active)
    def _():
        x = x_ref[...].astype(BF16)
        hg = jnp.dot(x, wg_s[...], preferred_element_type=F32) + bg_ref[0]
        hu = jnp.dot(x, wu_s[...], preferred_element_type=F32) + bu_ref[0]
        g = jnp.minimum(hg, SWIGLU_LIMIT)
        u = jnp.clip(hu, -SWIGLU_LIMIT, SWIGLU_LIMIT)
        a = (u + 1.0) * (g * jax.nn.sigmoid(SWIGLU_ALPHA * g))
        y_ref[...] = jnp.dot(a.astype(BF16), wd_s[...], preferred_element_type=F32) + bd_ref[0]

    @pl.when(jnp.logical_not(active))
    def _():
        y_ref[...] = jnp.zeros_like(y_ref)


def experts(xs, block_e, n_used, w_gu, bg, bu, w_down, bd):
    cap, d = xs.shape
    bm = BM
    _, _, f2 = w_gu.shape
    f = f2 // 2
    n_blocks = cap // bm

    def blk(m, be, nu):
        return jnp.minimum(m, nu[0] - 1)

    def exp_of(m, be, nu):
        return be[blk(m, be, nu)]

    per_expert = lambda shape: pl.BlockSpec(shape, lambda m, be, nu: (exp_of(m, be, nu), 0, 0))
    return pl.pallas_call(
        _expert_kernel,
        grid_spec=pltpu.PrefetchScalarGridSpec(
            num_scalar_prefetch=2,
            grid=(n_blocks,),
            in_specs=[pl.BlockSpec((bm, d), lambda m, be, nu: (blk(m, be, nu), 0)),
                      per_expert((1, d, f2)), per_expert((1, 1, f)), per_expert((1, 1, f)),
                      per_expert((1, f, d)), per_expert((1, 1, d)),
                      pl.BlockSpec((256, 256), lambda m, be, nu: (0, 0))],
            out_specs=pl.BlockSpec((bm, d), lambda m, be, nu: (m, 0)),
            scratch_shapes=[pltpu.VMEM((d, f), BF16), pltpu.VMEM((d, f), BF16), pltpu.VMEM((f, d), BF16)]),
        out_shape=jax.ShapeDtypeStruct((cap, d), F32),
        compiler_params=_cparams("arbitrary"),
        name="moe_experts",
    )(block_e, n_used, xs, w_gu, bg, bu, w_down, bd, _gate_up_selector())


def _combine_kernel(dest_ref, y_ref, x_ref, gt_ref, gate_ref, fg_ref, o_ref, buf, sems, *, tg, final_norm):
    s = pl.program_id(0)
    n_tiles = pl.num_programs(0) - 1
    cur = s % 2

    def row_copy(which, slot, k, t):
        return pltpu.make_async_copy(y_ref.at[pl.ds(slot, 1)], buf.at[which, k, pl.ds(t, 1)], sems.at[which])

    @pl.when(s < n_tiles)
    def _():
        def issue(g, c):
            for j in range(ROW_UNROLL):
                t = g * ROW_UNROLL + j
                for k in range(TOP_K):
                    row_copy(cur, dest_ref[t * TOP_K + k], k, t).start(priority=k % 2)
            return c

        lax.fori_loop(0, tg // ROW_UNROLL, issue, 0)

    @pl.when(s > 0)
    def _():
        prev = 1 - cur

        def drain(g, c):
            for _ in range(ROW_UNROLL * TOP_K):
                row_copy(prev, 0, 0, 0).wait()
            return c

        lax.fori_loop(0, tg // ROW_UNROLL, drain, 0)

        acc = gate_ref[:, 0:1] * buf[prev, 0]
        for k in range(1, TOP_K):
            acc += gate_ref[:, k:k + 1] * buf[prev, k]
        xn = x_ref[...] + gt_ref[0] * acc
        if final_norm:
            xn = _rms(xn) * fg_ref[...]
        o_ref[...] = xn


def combine(x_all, mods, y, dest_flat, gates, final_g, *, n_rows, seq, n_batch, final_norm):
    d = x_all.shape[1]
    tg = TG
    n_tiles = n_rows // tg

    def issued(s):
        return jnp.minimum(s, n_tiles - 1)

    def finished(s):
        return jnp.maximum(s - 1, 0)

    def mod_row(s):
        return jnp.minimum((finished(s) * tg) // seq, n_batch)

    return pl.pallas_call(
        functools.partial(_combine_kernel, tg=tg, final_norm=final_norm),
        grid=(n_tiles + 1,),
        in_specs=[pl.BlockSpec((tg * TOP_K,), lambda s: (issued(s),), memory_space=pltpu.SMEM),
                  pl.BlockSpec(memory_space=pl.ANY),
                  pl.BlockSpec((tg, d), lambda s: (finished(s), 0)),
                  pl.BlockSpec((1, 1, d), lambda s: (mod_row(s), 0, 5)),
                  pl.BlockSpec((tg, 128), lambda s: (finished(s), 0)),
                  pl.BlockSpec((1, d), lambda s: (0, 0))],
        out_specs=pl.BlockSpec((tg, d), lambda s: (finished(s), 0)),
        out_shape=jax.ShapeDtypeStruct((n_rows, d), F32),
        scratch_shapes=[pltpu.VMEM((2, TOP_K, tg, d), F32), pltpu.SemaphoreType.DMA((2,))],
        compiler_params=_cparams("arbitrary"),
        name="moe_combine",
    )(dest_flat, y, x_all, mods, gates, final_g)


def _rope_tables(seq, tm):
    half = HEAD_DIM // 2
    inv = ROPE_THETA ** (-jnp.arange(0, half, 2, dtype=F32) / half)
    t = jnp.arange(seq)
    lane = np.arange(128)
    dd = lane % HEAD_DIM
    use_col = (dd // half) == 1
    j = dd % (half // 2)
    second = (dd % half) >= (half // 2)
    pos = jnp.where(jnp.asarray(use_col)[None, :], (t % GRID_W)[:, None], (t // GRID_W)[:, None]).astype(F32)
    ang = pos * inv[jnp.asarray(j)][None, :]
    cos = jnp.cos(ang)
    sin = jnp.sin(ang) * jnp.where(jnp.asarray(second), 1.0, -1.0)[None, :]
    cos = jnp.concatenate([cos, jnp.ones((tm, 128), F32)], axis=0)
    sin = jnp.concatenate([sin, jnp.zeros((tm, 128), F32)], axis=0)
    return cos, sin


def _moe_plan(idx, rank, counts, n_exp, n_rows):
    bm = BM
    nk = n_rows * TOP_K
    n_blocks = -(-nk // bm) + n_exp
    counts = counts.reshape(n_exp)
    padded = (counts + bm - 1) // bm * bm
    pend = jnp.cumsum(padded)
    pstart = pend - padded
    dest = (pstart[idx] + rank).reshape(nk).astype(jnp.int32)
    block_start = jnp.arange(n_blocks, dtype=jnp.int32) * bm
    block_e = jnp.minimum(jnp.sum(pend[None, :] <= block_start[:, None], axis=1), n_exp - 1).astype(jnp.int32)
    n_used = (pend[-1:] // bm).astype(jnp.int32)
    return dest, block_e, n_used, (pstart + counts).astype(jnp.int32), pend.astype(jnp.int32), n_blocks * bm


def kernel(x, c, ctx, c_ctx, ada_w, ada_b, norm1_g, norm2_g, w_in, conv_w, na_rpb, q_norm_g, k_norm_g,
           w_br_conv, w_br_na, w_br_gqa, w_out, router_w, router_b, w_gu, b_gu, w_down, b_down, final_g):
    n_batch, seq, d = x.shape
    n_ctx = ctx.shape[1]
    depth = ada_w.shape[0]
    n_exp = router_w.shape[-1]
    n_lat = n_batch * seq
    n_all = n_lat + n_batch * n_ctx
    assert n_batch + 1 <= MOD_ROWS and seq % TM == 0 and (n_batch * n_ctx) % TM == 0

    cond = jnp.zeros((MOD_ROWS, d), F32).at[:n_batch].set(c).at[n_batch].set(c_ctx)
    mods_all = ada_mods(cond, ada_w, ada_b)
    cos_t, sin_t = _rope_tables(seq, TM)
    hsum = jnp.asarray(np.kron(np.eye(256 // HEAD_DIM), np.ones((HEAD_DIM, HEAD_DIM))), BF16)

    f = w_gu.shape[-1] // 2
    w_gu_flat = w_gu.reshape(depth * n_exp, d, 2 * f)
    bg = b_gu[..., 0::2].reshape(depth * n_exp, 1, f)
    bu = b_gu[..., 1::2].reshape(depth * n_exp, 1, f)
    w_down_flat = w_down.reshape(depth * n_exp, f, d)
    bd = b_down.reshape(depth * n_exp, 1, d)

    x_parts = (x.reshape(n_lat, d), ctx.reshape(n_batch * n_ctx, d))
    for l in range(depth):
        last = l == depth - 1
        mods = mods_all[l].reshape(MOD_ROWS, 1, 6 * d)
        qg = jnp.tile(q_norm_g[l], 256 // HEAD_DIM).reshape(1, 256)
        kg = jnp.tile(k_norm_g[l], 128 // HEAD_DIM).reshape(1, 128)
        conv, naq, nak, nav, gq, gk, gv, gates = in_projection(
            x_parts, mods, norm1_g[l].reshape(1, d), w_in[l].astype(BF16), cos_t, sin_t, qg, kg, hsum,
            seq=seq, n_lat=n_lat, n_batch=n_batch)

        y_na = neighborhood_attention(naq, nak, nav, na_bias_table(na_rpb[l] * LOG2E, seq),
                                      seq=seq, ctx=n_ctx, n_batch=n_batch)

        def kv_all(t):
            return jnp.concatenate([t[:n_lat].reshape(n_batch, seq, GQA_KV_WIDTH),
                                    t[n_lat:].reshape(n_batch, n_ctx, GQA_KV_WIDTH)], axis=1)

        s_all = seq + n_ctx
        kt_all = (kv_all(gk).transpose(0, 2, 1) * GQA_K_FP8_SCALE).astype(FP8)
        kt_all = kt_all.reshape(n_batch, GQA_KV_HEADS, HEAD_DIM, s_all)
        gv_all = kv_all(gv)
        v_tail = jnp.concatenate([jnp.ones((n_batch, s_all, 1), BF16),
                                  jnp.zeros((n_batch, s_all, HEAD_DIM - 1), BF16)], axis=-1)
        v_all = jnp.stack([jnp.concatenate([gv_all[..., kv * HEAD_DIM:(kv + 1) * HEAD_DIM], v_tail], axis=-1)
                           for kv in range(GQA_KV_HEADS)], axis=1)
        y_gqa = gqa_attention(gq, kt_all, v_all, seq=seq, n_batch=n_batch)

        n_rows = n_lat if last else n_all
        y_na_parts, y_gqa_parts = (y_na,), (y_gqa,)
        if not last:
            yc_na, yc_gqa = ctx_attention(naq, nak, nav, gq, gk, gv, ctx=n_ctx, n_lat=n_lat, n_batch=n_batch)
            y_na_parts, y_gqa_parts = (y_na, yc_na), (y_gqa, yc_gqa)
        x_all = merge_branches(x_parts, mods, conv, y_na_parts, y_gqa_parts, gates, conv_w[l],
                               w_br_conv[l].astype(BF16), w_br_na[l].astype(BF16), w_br_gqa[l].astype(BF16),
                               w_out[l].astype(BF16), n_rows=n_rows, seq=seq, ctx=n_ctx, n_lat=n_lat,
                               n_batch=n_batch)

        h2, idx, gate_w, rank, counts = router(x_all, mods, norm2_g[l].reshape(1, d), router_w[l], router_b[l],
                                               n_rows=n_rows, seq=seq, n_batch=n_batch)
        dest, block_e, n_used, fill_lo, fill_hi, cap = _moe_plan(
            idx[:, :TOP_K], rank[:, :TOP_K], counts, n_exp, n_rows)
        xs = dispatch(h2, dest, fill_lo, fill_hi, cap)
        y = experts(xs, block_e + l * n_exp, n_used, w_gu_flat, bg, bu, w_down_flat, bd)
        x_all = combine(x_all, mods, y, dest, gate_w, final_g.reshape(1, d),
                        n_rows=n_rows, seq=seq, n_batch=n_batch, final_norm=last)
        x_parts = (x_all,)
    return x_all.reshape(n_batch, seq, d)
```
